```python
import math
import jax, jax.numpy as jnp
from jax import lax
import numpy as np

D_MODEL = 4096
BATCH = 8
SEQ = 2048
DEPTH = 2

HEAD_DIM = 128
CONV_WIDTH = D_MODEL // 2
CONV_K = 3
MOBA_HEADS = (D_MODEL // 2) // HEAD_DIM
MOBA_WIDTH = MOBA_HEADS * HEAD_DIM
MOBA_BLOCK = 256
MOBA_TOPK = 3
MOBA_Q_CHUNK = 32
SB_HEADS = D_MODEL // HEAD_DIM
SB_WIDTH = SB_HEADS * HEAD_DIM
SB_Q_BLOCK = 128
N_MEM = 256
XATTN_HEADS = 4
XATTN_HEAD_DIM = D_MODEL // 16
XATTN_WIDTH = XATTN_HEADS * XATTN_HEAD_DIM
REL_BUCKETS = 32
REL_MAX_DIST = 128
LN_EPS = 1e-5
NEG_INF = -1e30
DEEPNORM_ALPHA = (2 * DEPTH) ** 0.25
DEEPNORM_BETA = (8 * DEPTH) ** -0.25

kernel_name = "hybrid_conv_moba_stickbreak_deepnorm"


def layer_norm(x, g, b):
    xf = x.astype(jnp.float32)
    mu = jnp.mean(xf, axis=-1, keepdims=True)
    var = jnp.mean(jnp.square(xf - mu), axis=-1, keepdims=True)
    return ((xf - mu) * lax.rsqrt(var + LN_EPS)).astype(x.dtype) * g + b


def t5_causal_bucket(dist):
    n = jnp.maximum(dist, 0)
    max_exact = REL_BUCKETS // 2
    nf = jnp.maximum(n, 1).astype(jnp.float32)
    large = max_exact + (jnp.log(nf / max_exact) / math.log(REL_MAX_DIST / max_exact)
                         * (REL_BUCKETS - max_exact)).astype(jnp.int32)
    large = jnp.minimum(large, REL_BUCKETS - 1)
    return jnp.where(n < max_exact, n, large)


def short_conv(h, w):
    c = h.shape[-1]
    return lax.conv_general_dilated(h, w[:, None, :], window_strides=(1,),
                                    padding=[(CONV_K - 1, 0)],
                                    dimension_numbers=('NWC', 'WIO', 'NWC'),
                                    feature_group_count=c)


def moba_attention(q, k, v, rel_bias):
    b, s, h, dh = q.shape
    nb = -(-s // MOBA_BLOCK)
    pad = nb * MOBA_BLOCK - s
    scale = dh ** -0.5
    qh = q.transpose(0, 2, 1, 3)

    def to_blocks(t):
        t = jnp.pad(t, ((0, 0), (0, pad), (0, 0), (0, 0)))
        return t.reshape(b, nb, MOBA_BLOCK, h, dh).transpose(0, 3, 1, 2, 4)

    kbh = to_blocks(k)
    vbh = to_blocks(v)
    rb = rel_bias.T
    qpos = jnp.arange(s, dtype=jnp.int32)
    qblk = qpos // MOBA_BLOCK
    n_sel = min(MOBA_TOPK, nb - 1)
    if n_sel > 0:
        kbar = jnp.mean(kbh.astype(jnp.float32), axis=3)
        gate = jnp.einsum('bhsd,bhnd->bhsn', qh.astype(jnp.float32), kbar)
        past = jnp.arange(nb, dtype=jnp.int32)[None, :] < qblk[:, None]
        gate = jnp.where(past, gate, NEG_INF)
        _, sel = lax.top_k(gate, n_sel)
        sel_valid = sel < qblk[:, None]
    bi = jnp.arange(b)[:, None, None, None]
    hi = jnp.arange(h)[None, :, None, None]
    blk_off = jnp.arange(MOBA_BLOCK, dtype=jnp.int32)
    n_keys_sel = n_sel * MOBA_BLOCK

    def chunk(c):
        q0 = c * MOBA_Q_CHUNK
        qc = lax.dynamic_slice_in_dim(qh, q0, MOBA_Q_CHUNK, axis=2)
        qp = q0 + jnp.arange(MOBA_Q_CHUNK, dtype=jnp.int32)
        ob = q0 // MOBA_BLOCK
        k_own = lax.dynamic_index_in_dim(kbh, ob, axis=2, keepdims=False)
        v_own = lax.dynamic_index_in_dim(vbh, ob, axis=2, keepdims=False)
        kp_own = ob * MOBA_BLOCK + blk_off
        bkt_own = t5_causal_bucket(qp[:, None] - kp_own[None, :])
        lg_own = jnp.einsum('bhqd,bhkd->bhqk', qc, k_own).astype(jnp.float32) * scale + rb[:, bkt_own]
        lg_own = jnp.where(kp_own[None, :] <= qp[:, None], lg_own, NEG_INF)
        if n_sel == 0:
            p = jax.nn.softmax(lg_own, axis=-1).astype(v.dtype)
            return jnp.einsum('bhqk,bhkd->bhqd', p, v_own)
        sc = lax.dynamic_slice_in_dim(sel, q0, MOBA_Q_CHUNK, axis=2)
        vc = lax.dynamic_slice_in_dim(sel_valid, q0, MOBA_Q_CHUNK, axis=2)
        k_sel = kbh[bi, hi, sc]
        v_sel = vbh[bi, hi, sc]
        kp_sel = sc[..., None] * MOBA_BLOCK + blk_off
        bkt_sel = t5_causal_bucket(qp[None, None, :, None, None] - kp_sel)
        lg_sel = (jnp.einsum('bhqd,bhqnkd->bhqnk', qc, k_sel).astype(jnp.float32) * scale
                  + rb[hi[..., None], bkt_sel])
        lg_sel = jnp.where(vc[..., None], lg_sel, NEG_INF).reshape(b, h, MOBA_Q_CHUNK, n_keys_sel)
        p = jax.nn.softmax(jnp.concatenate([lg_sel, lg_own], axis=-1), axis=-1).astype(v.dtype)
        v_sel = v_sel.reshape(b, h, MOBA_Q_CHUNK, n_keys_sel, dh)
        return (jnp.einsum('bhqm,bhqmd->bhqd', p[..., :n_keys_sel], v_sel)
                + jnp.einsum('bhqk,bhkd->bhqd', p[..., n_keys_sel:], v_own))

    out = lax.map(chunk, jnp.arange(s // MOBA_Q_CHUNK, dtype=jnp.int32))
    return out.transpose(1, 0, 3, 2, 4).reshape(b, s, h, dh)


def stick_breaking_attention(q, k, v):
    b, s, h, dh = q.shape
    scale = dh ** -0.5
    outs = []
    for i in range(s // SB_Q_BLOCK):
        q0 = i * SB_Q_BLOCK
        q1 = q0 + SB_Q_BLOCK
        z = jnp.einsum('bqhd,bkhd->bhqk', q[:, q0:q1], k[:, :q1]).astype(jnp.float32) * scale
        qp = jnp.arange(q0, q1, dtype=jnp.int32)
        kp = jnp.arange(q1, dtype=jnp.int32)
        strict = kp[None, :] < qp[:, None]
        log_1m_beta = jnp.where(strict, jax.nn.log_sigmoid(-z), 0.0)
        rem = lax.cumsum(log_1m_beta, axis=3, reverse=True) - log_1m_beta
        w = jnp.where(strict, jnp.exp(jax.nn.log_sigmoid(z) + rem), 0.0).astype(v.dtype)
        outs.append(jnp.einsum('bhqk,bkhd->bqhd', w, v[:, :q1]))
    return jnp.concatenate(outs, axis=1)


def conv_moba_mixer(x, w_in, conv_w, w_out, rel_bias):
    b, s, _ = x.shape
    widths = [CONV_WIDTH] * 4 + [MOBA_WIDTH] * 4
    splits = np.cumsum(widths)[:-1].tolist()
    h, b_gate, c_gate, g_a, q, k, v, g_b = jnp.split(x @ w_in, splits, axis=-1)
    y_a = b_gate * short_conv(c_gate * h, conv_w) * jax.nn.silu(g_a)
    hd = (b, s, MOBA_HEADS, HEAD_DIM)
    y_b = moba_attention(q.reshape(hd), k.reshape(hd), v.reshape(hd), rel_bias).reshape(b, s, MOBA_WIDTH)
    y_b = y_b * jax.nn.silu(g_b)
    return jnp.concatenate([y_a, y_b], axis=-1) @ w_out


def stick_breaking_mixer(x, w_in, w_out):
    b, s, _ = x.shape
    q, k, v, g = jnp.split(x @ w_in, 4, axis=-1)
    hd = (b, s, SB_HEADS, HEAD_DIM)
    y = stick_breaking_attention(q.reshape(hd), k.reshape(hd), v.reshape(hd)).reshape(b, s, SB_WIDTH)
    return (y * jax.nn.silu(g)) @ w_out


def memory_cross_attention(x, mem, w_xq, w_xkv, w_xo):
    b, s, _ = x.shape
    m = mem.shape[1]
    q, g = jnp.split(x @ w_xq, 2, axis=-1)
    k, v = jnp.split(mem @ w_xkv, 2, axis=-1)
    q = q.reshape(b, s, XATTN_HEADS, XATTN_HEAD_DIM)
    k = k.reshape(b, m, XATTN_HEADS, XATTN_HEAD_DIM)
    v = v.reshape(b, m, XATTN_HEADS, XATTN_HEAD_DIM)
    lg = jnp.einsum('bshd,bmhd->bhsm', q, k).astype(jnp.float32) * (XATTN_HEAD_DIM ** -0.5)
    p = jax.nn.softmax(lg, axis=-1).astype(v.dtype)
    o = jnp.einsum('bhsm,bmhd->bshd', p, v).reshape(b, s, XATTN_WIDTH)
    return (o * jax.nn.silu(g)) @ w_xo


def setup_inputs(seed: int = 0) -> dict:
    key = jax.random.key(seed)
    ks = jax.random.split(key, 32)

    def dense(k, fan_in, fan_out, scale=1.0):
        return jax.random.normal(k, (fan_in, fan_out), jnp.float32) * (scale * fan_in ** -0.5)

    def gain(k):
        return 1.0 + 0.02 * jax.random.normal(k, (D_MODEL,), jnp.float32)

    def bias(k):
        return 0.02 * jax.random.normal(k, (D_MODEL,), jnp.float32)

    even_in = 4 * CONV_WIDTH + 4 * MOBA_WIDTH
    return {
        'x': jax.random.normal(ks[0], (BATCH, SEQ, D_MODEL), jnp.float32),
        'mem': jax.random.normal(ks[1], (BATCH, N_MEM, D_MODEL), jnp.float32),
        'rel_bias': 0.5 * jax.random.normal(ks[2], (REL_BUCKETS, MOBA_HEADS), jnp.float32),
        'w_in_0': dense(ks[3], D_MODEL, even_in),
        'conv_w_0': jax.random.normal(ks[4], (CONV_K, CONV_WIDTH), jnp.float32) * CONV_K ** -0.5,
        'w_out_0': dense(ks[5], CONV_WIDTH + MOBA_WIDTH, D_MODEL, DEEPNORM_BETA),
        'ln1_g_0': gain(ks[6]),
        'ln1_b_0': bias(ks[7]),
        'w_xq_0': dense(ks[8], D_MODEL, 2 * XATTN_WIDTH),
        'w_xkv_0': dense(ks[9], D_MODEL, 2 * XATTN_WIDTH),
        'w_xo_0': dense(ks[10], XATTN_WIDTH, D_MODEL, DEEPNORM_BETA),
        'ln2_g_0': gain(ks[11]),
        'ln2_b_0': bias(ks[12]),
        'w_in_1': dense(ks[13], D_MODEL, 4 * SB_WIDTH),
        'w_out_1': dense(ks[14], SB_WIDTH, D_MODEL, DEEPNORM_BETA),
        'ln1_g_1': gain(ks[15]),
        'ln1_b_1': bias(ks[16]),
        'w_xq_1': dense(ks[17], D_MODEL, 2 * XATTN_WIDTH),
        'w_xkv_1': dense(ks[18], D_MODEL, 2 * XATTN_WIDTH),
        'w_xo_1': dense(ks[19], XATTN_WIDTH, D_MODEL, DEEPNORM_BETA),
        'ln2_g_1': gain(ks[20]),
        'ln2_b_1': bias(ks[21]),
    }


def reference(x, mem, rel_bias,
              w_in_0, conv_w_0, w_out_0, ln1_g_0, ln1_b_0, w_xq_0, w_xkv_0, w_xo_0, ln2_g_0, ln2_b_0,
              w_in_1, w_out_1, ln1_g_1, ln1_b_1, w_xq_1, w_xkv_1, w_xo_1, ln2_g_1, ln2_b_1):
    mixer_params = [(w_in_0, conv_w_0, w_out_0), (w_in_1, w_out_1)]
    ln_mix = [(ln1_g_0, ln1_b_0), (ln1_g_1, ln1_b_1)]
    xattn = [(w_xq_0, w_xkv_0, w_xo_0), (w_xq_1, w_xkv_1, w_xo_1)]
    ln_mem = [(ln2_g_0, ln2_b_0), (ln2_g_1, ln2_b_1)]
    for layer in range(DEPTH):
        if layer % 2 == 0:
            w_in, conv_w, w_out = mixer_params[layer]
            y = conv_moba_mixer(x, w_in, conv_w, w_out, rel_bias)
        else:
            w_in, w_out = mixer_params[layer]
            y = stick_breaking_mixer(x, w_in, w_out)
        g1, b1 = ln_mix[layer]
        x = layer_norm(DEEPNORM_ALPHA * x + y, g1, b1)
        w_xq, w_xkv, w_xo = xattn[layer]
        g2, b2 = ln_mem[layer]
        x = layer_norm(DEEPNORM_ALPHA * x + memory_cross_attention(x, mem, w_xq, w_xkv, w_xo), g2, b2)
    return x
```

```python
import functools
import math

import numpy as np
import jax
import jax.numpy as jnp
from jax import lax
from jax.experimental import pallas as pl
from jax.experimental.pallas import tpu as pltpu

DEPTH = 2
HEAD_DIM = 128
CONV_K = 3
MOBA_BLOCK = 256
MOBA_TOPK = 3
XATTN_HEADS = 4
REL_BUCKETS = 32
REL_MAX_DIST = 128
LN_EPS = 1e-5
NEG_INF = -1e30
DEEPNORM_ALPHA = (2 * DEPTH) ** 0.25

SB_BLOCK = 256
VMEM_LIMIT = 56 * 1024 * 1024

_NT = (((1,), (1,)), ((), ()))


def _cparams(n_axes):
    return pltpu.CompilerParams(dimension_semantics=("arbitrary",) * n_axes,
                                vmem_limit_bytes=VMEM_LIMIT)


def _silu(g):
    return g * jax.nn.sigmoid(g)


def _split_bf16(x):
    hi = x.astype(jnp.bfloat16)
    lo = (x - hi.astype(jnp.float32)).astype(jnp.bfloat16)
    return hi, lo


def _mm_kernel(a_ref, w_ref, o_ref):
    o_ref[...] = jnp.dot(a_ref[...], w_ref[...],
                         preferred_element_type=jnp.float32).astype(o_ref.dtype)


def _matmul(a, w, n_start, n_size, out_dtype, bm=1024, bn=1024):
    m, k = a.shape
    bm = min(bm, m)
    bn = math.gcd(math.gcd(bn, n_size), n_start)
    assert m % bm == 0 and bn % 128 == 0
    joff = n_start // bn
    return pl.pallas_call(
        _mm_kernel,
        out_shape=jax.ShapeDtypeStruct((m, n_size), out_dtype),
        grid=(m // bm, n_size // bn),
        in_specs=[pl.BlockSpec((bm, k), lambda i, j: (i, 0)),
                  pl.BlockSpec((k, bn), lambda i, j: (0, j + joff))],
        out_specs=pl.BlockSpec((bm, bn), lambda i, j: (i, j)),
        compiler_params=_cparams(2),
        name="matmul",
    )(a, w)


def _ln_kernel(x_ref, y_ref, g_ref, b_ref, of_ref, ob_ref):
    r = DEEPNORM_ALPHA * x_ref[...] + y_ref[...]
    mu = jnp.mean(r, axis=-1, keepdims=True)
    d = r - mu
    var = jnp.mean(d * d, axis=-1, keepdims=True)
    o = d * lax.rsqrt(var + LN_EPS) * g_ref[...] + b_ref[...]
    of_ref[...] = o
    ob_ref[...] = o.astype(jnp.bfloat16)


def _residual_ln(x, y, g, b, bm=256):
    m, d = x.shape
    bm = min(bm, m)
    row = pl.BlockSpec((bm, d), lambda i: (i, 0))
    vec = pl.BlockSpec((1, d), lambda i: (0, 0))
    return pl.pallas_call(
        _ln_kernel,
        out_shape=(jax.ShapeDtypeStruct((m, d), jnp.float32),
                   jax.ShapeDtypeStruct((m, d), jnp.bfloat16)),
        grid=(m // bm,),
        in_specs=[row, row, vec, vec],
        out_specs=(row, row),
        compiler_params=_cparams(1),
        name="residual_ln",
    )(x, y, g.reshape(1, d), b.reshape(1, d))


def _convgate_kernel(h_ref, b_ref, c_ref, g_ref, w_ref, o_ref):
    u = c_ref[0] * h_ref[0]
    row = lax.broadcasted_iota(jnp.int32, u.shape, 0)
    u1 = jnp.where(row >= 1, pltpu.roll(u, 1, 0), 0.0)
    u2 = jnp.where(row >= 2, pltpu.roll(u, 2, 0), 0.0)
    w = w_ref[...]
    conv = w[0:1, :] * u2 + w[1:2, :] * u1 + w[2:3, :] * u
    o_ref[0] = (b_ref[0] * conv * _silu(g_ref[0])).astype(o_ref.dtype)


def _conv_gate(p, conv_w, width, bc=256):
    bsz, s, _ = p.shape
    bc = min(bc, width)
    nc = width // bc

    def col(group):
        return pl.BlockSpec((1, s, bc), lambda b, j: (b, 0, j + group * nc))

    return pl.pallas_call(
        _convgate_kernel,
        out_shape=jax.ShapeDtypeStruct((bsz, s, width), jnp.bfloat16),
        grid=(bsz, nc),
        in_specs=[col(0), col(1), col(2), col(3),
                  pl.BlockSpec((CONV_K, bc), lambda b, j: (0, j))],
        out_specs=pl.BlockSpec((1, s, bc), lambda b, j: (b, 0, j)),
        compiler_params=_cparams(2),
        name="conv_gate",
    )(p, p, p, p, conv_w)


def _t5_bucket_thresholds():
    d = np.arange(0, 4 * REL_MAX_DIST)
    max_exact = REL_BUCKETS // 2
    nf = np.maximum(d, 1).astype(np.float32)
    large = max_exact + (np.log(nf / np.float32(max_exact)) / np.float32(math.log(REL_MAX_DIST / max_exact))
                         * np.float32(REL_BUCKETS - max_exact)).astype(np.int32)
    large = np.minimum(large, REL_BUCKETS - 1)
    bucket = np.where(d < max_exact, d, large)
    assert np.all(np.diff(bucket) >= 0) and bucket[-1] == REL_BUCKETS - 1
    return [int(np.argmax(bucket >= b)) for b in range(REL_BUCKETS)]


_BUCKET_THR = _t5_bucket_thresholds()


def _moba_kernel(rb_ref, q_ref, k_ref, v_ref, g_ref, o_ref, t0_ref, t1_ref, *, nb):
    blk = MOBA_BLOCK
    h = pl.program_id(0)
    scale = HEAD_DIM ** -0.5
    qo = lax.broadcasted_iota(jnp.int32, (blk, blk), 0)
    ko = lax.broadcasted_iota(jnp.int32, (blk, blk), 1)

    @pl.when(pl.program_id(1) == 0)
    def _():
        def table(dist):
            t = jnp.full(dist.shape, rb_ref[h, REL_BUCKETS - 1], jnp.float32)
            for bkt in range(REL_BUCKETS - 2, -1, -1):
                t = jnp.where(dist < _BUCKET_THR[bkt + 1], rb_ref[h, bkt], t)
            return t
        t0_ref[...] = table(qo - ko)
        t1_ref[...] = table(qo - ko + blk)

    rb_far = rb_ref[h, REL_BUCKETS - 1]
    assert _BUCKET_THR[REL_BUCKETS - 1] <= blk + 1

    kbar = jnp.concatenate(
        [jnp.mean(k_ref[0, j * blk:(j + 1) * blk, :].astype(jnp.float32), axis=0, keepdims=True)
         for j in range(nb)] + [jnp.zeros((16 - nb, HEAD_DIM), jnp.float32)], axis=0)
    kbar_hi, kbar_lo = _split_bf16(kbar)

    eye = jnp.where(qo == ko, 1.0, 0.0).astype(jnp.bfloat16)
    jidx = lax.broadcasted_iota(jnp.int32, (16, blk), 0)

    for i in range(nb):
        q_i = q_ref[0, i * blk:(i + 1) * blk, :]
        nk = (i + 1) * blk
        s = lax.dot_general(q_i, k_ref[0, 0:nk, :], _NT,
                            preferred_element_type=jnp.float32) * scale
        pieces = []
        if i > 0:
            if i > MOBA_TOPK:
                gate = (lax.dot_general(kbar_hi, q_i, _NT, preferred_element_type=jnp.float32)
                        + lax.dot_general(kbar_lo, q_i, _NT, preferred_element_type=jnp.float32))
                cnt = jnp.zeros((16, blk), jnp.float32)
                for jp in range(i):
                    row = gate[jp:jp + 1, :]
                    beats = jnp.where(row > gate, 1.0,
                                      jnp.where(row == gate, jnp.where(jidx > jp, 1.0, 0.0), 0.0))
                    cnt = cnt + beats
                sel_t = jnp.where(cnt < MOBA_TOPK, jnp.where(jidx < i, 1.0, 0.0), 0.0)
                sel_t = jnp.concatenate([sel_t, jnp.zeros((128 - 16, blk), jnp.float32)], axis=0)
                sel = lax.dot_general(eye, sel_t.astype(jnp.bfloat16), _NT,
                                      preferred_element_type=jnp.float32)
                er = lax.broadcasted_iota(jnp.int32, (128, i * blk), 0)
                ec = lax.broadcasted_iota(jnp.int32, (128, i * blk), 1)
                expand = jnp.where((ec >= er * blk) & (ec < er * blk + blk), 1.0, 0.0)
                keep = jnp.dot(sel.astype(jnp.bfloat16), expand.astype(jnp.bfloat16),
                               preferred_element_type=jnp.float32) > 0.5
            for j in range(i):
                sj = s[:, j * blk:(j + 1) * blk]
                lg = sj + (t1_ref[...] if j == i - 1 else rb_far)
                if i > MOBA_TOPK:
                    lg = jnp.where(keep[:, j * blk:(j + 1) * blk], lg, NEG_INF)
                pieces.append(lg)
        pieces.append(jnp.where(ko <= qo, s[:, i * blk:nk] + t0_ref[...], NEG_INF))
        lg = pieces[0] if len(pieces) == 1 else jnp.concatenate(pieces, axis=-1)
        m = jnp.max(lg, axis=-1, keepdims=True)
        p = jnp.exp(lg - m)
        l = jnp.sum(p, axis=-1, keepdims=True)
        o = jnp.dot(p.astype(jnp.bfloat16), v_ref[0, 0:nk, :],
                    preferred_element_type=jnp.float32) / l
        o_ref[0, i * blk:(i + 1) * blk, :] = (o * _silu(g_ref[0, i * blk:(i + 1) * blk, :])
                                              ).astype(o_ref.dtype)


def _moba(qkv, g, rel_bias_t, heads):
    bsz, s, _ = qkv.shape
    assert s % MOBA_BLOCK == 0
    nb = s // MOBA_BLOCK
    assert MOBA_TOPK < nb <= 16

    def col(group):
        return pl.BlockSpec((1, s, HEAD_DIM), lambda h, b: (b, 0, h + group * heads))

    return pl.pallas_call(
        functools.partial(_moba_kernel, nb=nb),
        out_shape=jax.ShapeDtypeStruct((bsz, s, heads * HEAD_DIM), jnp.bfloat16),
        grid=(heads, bsz),
        in_specs=[pl.BlockSpec(memory_space=pltpu.SMEM), col(0), col(1), col(2),
                  pl.BlockSpec((1, s, HEAD_DIM), lambda h, b: (b, 0, h))],
        out_specs=pl.BlockSpec((1, s, HEAD_DIM), lambda h, b: (b, 0, h)),
        scratch_shapes=[pltpu.VMEM((MOBA_BLOCK, MOBA_BLOCK), jnp.float32),
                        pltpu.VMEM((MOBA_BLOCK, MOBA_BLOCK), jnp.float32)],
        compiler_params=_cparams(2),
        name="moba",
    )(rel_bias_t, qkv, qkv, qkv, g)


def _sb_kernel(q_ref, k_ref, v_ref, g_ref, o_ref, *, nq):
    t = SB_BLOCK
    scale = HEAD_DIM ** -0.5
    r = lax.broadcasted_iota(jnp.int32, (t, t), 0)
    c = lax.broadcasted_iota(jnp.int32, (t, t), 1)
    strict = c < r
    tri = jnp.where(r > c, 1.0, 0.0).astype(jnp.bfloat16)
    tri2 = jnp.concatenate([tri, tri], axis=0)

    def tile(q_i, j0, carry, acc, diag):
        z = lax.dot_general(q_i, k_ref[0, pl.ds(j0, t), :], _NT,
                            preferred_element_type=jnp.float32) * scale
        l = -(jnp.maximum(z, 0.0) + jnp.log1p(jnp.exp(-jnp.abs(z))))
        if diag:
            l = jnp.where(strict, l, 0.0)
        l_hi, l_lo = _split_bf16(l)
        rem = jnp.dot(jnp.concatenate([l_hi, l_lo], axis=-1), tri2,
                      preferred_element_type=jnp.float32) + carry
        w = jnp.exp(z + l + rem)
        if diag:
            w = jnp.where(strict, w, 0.0)
        acc = acc + jnp.dot(w.astype(jnp.bfloat16), v_ref[0, pl.ds(j0, t), :],
                            preferred_element_type=jnp.float32)
        carry = carry + jnp.sum(l, axis=-1, keepdims=True)
        return carry, acc

    def q_block(i, _):
        q0 = pl.multiple_of(i * t, t)
        q_i = q_ref[0, pl.ds(q0, t), :]
        carry, acc = tile(q_i, q0, jnp.zeros((t, 1), jnp.float32),
                          jnp.zeros((t, HEAD_DIM), jnp.float32), True)

        def past(n, state):
            j0 = pl.multiple_of((i - 1 - n) * t, t)
            return tile(q_i, j0, state[0], state[1], False)

        carry, acc = lax.fori_loop(0, i, past, (carry, acc))
        o_ref[0, pl.ds(q0, t), :] = (acc * _silu(g_ref[0, pl.ds(q0, t), :])).astype(o_ref.dtype)
        return 0

    lax.fori_loop(0, nq, q_block, 0)


def _stick_breaking(qkv, g, heads):
    bsz, s, _ = qkv.shape
    assert s % SB_BLOCK == 0

    def col(group):
        return pl.BlockSpec((1, s, HEAD_DIM), lambda h, b: (b, 0, h + group * heads))

    return pl.pallas_call(
        functools.partial(_sb_kernel, nq=s // SB_BLOCK),
        out_shape=jax.ShapeDtypeStruct((bsz, s, heads * HEAD_DIM), jnp.bfloat16),
        grid=(heads, bsz),
        in_specs=[col(0), col(1), col(2),
                  pl.BlockSpec((1, s, HEAD_DIM), lambda h, b: (b, 0, h))],
        out_specs=pl.BlockSpec((1, s, HEAD_DIM), lambda h, b: (b, 0, h)),
        compiler_params=_cparams(2),
        name="stick_breaking",
    )(qkv, qkv, qkv, g)


def _xattn_kernel(q_ref, g_ref, k_ref, v_ref, o_ref, *, dh):
    scale = dh ** -0.5
    for h in range(XATTN_HEADS):
        cols = slice(h * dh, (h + 1) * dh)
        s = lax.dot_general(q_ref[0, :, cols], k_ref[0, :, cols], _NT,
                            preferred_element_type=jnp.float32) * scale
        m = jnp.max(s, axis=-1, keepdims=True)
        p = jnp.exp(s - m)
        l = jnp.sum(p, axis=-1, keepdims=True)
        o = jnp.dot(p.astype(jnp.bfloat16), v_ref[0, :, cols],
                    preferred_element_type=jnp.float32) / l
        o_ref[0, :, cols] = (o * _silu(g_ref[0, :, cols])).astype(o_ref.dtype)


def _xattn(q, g, kv, bm=512):
    bsz, s, w = q.shape
    n_mem = kv.shape[1]
    bm = min(bm, s)
    row = pl.BlockSpec((1, bm, w), lambda b, i: (b, i, 0))
    return pl.pallas_call(
        functools.partial(_xattn_kernel, dh=w // XATTN_HEADS),
        out_shape=jax.ShapeDtypeStruct((bsz, s, w), jnp.bfloat16),
        grid=(bsz, s // bm),
        in_specs=[row, row,
                  pl.BlockSpec((1, n_mem, w), lambda b, i: (b, 0, 0)),
                  pl.BlockSpec((1, n_mem, w), lambda b, i: (b, 0, 1))],
        out_specs=row,
        compiler_params=_cparams(2),
        name="xattn",
    )(q, g, kv, kv)


def _memory_block(x_f32, x_bf16, mem_bf16, w_xq, w_xkv, w_xo, g2, b2, bsz, s):
    d = x_f32.shape[-1]
    xw = w_xo.shape[0]
    q = _matmul(x_bf16, w_xq, 0, xw, jnp.bfloat16)
    g = _matmul(x_bf16, w_xq, xw, xw, jnp.float32)
    kv = _matmul(mem_bf16, w_xkv, 0, 2 * xw, jnp.bfloat16)
    n_mem = mem_bf16.shape[0] // bsz
    o = _xattn(q.reshape(bsz, s, xw), g.reshape(bsz, s, xw), kv.reshape(bsz, n_mem, 2 * xw))
    y = _matmul(o.reshape(bsz * s, xw), w_xo, 0, d, jnp.float32)
    return _residual_ln(x_f32, y, g2, b2)


def kernel(x, mem, rel_bias, w_in_0, conv_w_0, w_out_0, ln1_g_0, ln1_b_0, w_xq_0, w_xkv_0, w_xo_0, ln2_g_0, ln2_b_0, w_in_1, w_out_1, ln1_g_1, ln1_b_1, w_xq_1, w_xkv_1, w_xo_1, ln2_g_1, ln2_b_1):
    bsz, s, d = x.shape
    bf = jnp.bfloat16
    xf = x.reshape(bsz * s, d)
    xb = xf.astype(bf)
    memb = mem.reshape(-1, d).astype(bf)

    cw = conv_w_0.shape[1]
    mw = w_in_0.shape[1] // 4 - cw
    heads = mw // HEAD_DIM
    w_in = w_in_0.astype(bf)
    p_conv = _matmul(xb, w_in, 0, 4 * cw, jnp.float32)
    qkv = _matmul(xb, w_in, 4 * cw, 3 * mw, bf)
    g_b = _matmul(xb, w_in, 4 * cw + 3 * mw, mw, jnp.float32)
    y_a = _conv_gate(p_conv.reshape(bsz, s, 4 * cw), conv_w_0, cw)
    y_b = _moba(qkv.reshape(bsz, s, 3 * mw), g_b.reshape(bsz, s, mw), rel_bias.T, heads)
    y_cat = jnp.concatenate([y_a, y_b], axis=-1).reshape(bsz * s, cw + mw)
    y = _matmul(y_cat, w_out_0.astype(bf), 0, d, jnp.float32)
    xf, xb = _residual_ln(xf, y, ln1_g_0, ln1_b_0)
    xf, xb = _memory_block(xf, xb, memb, w_xq_0.astype(bf), w_xkv_0.astype(bf), w_xo_0.astype(bf),
                           ln2_g_0, ln2_b_0, bsz, s)

    sw = w_out_1.shape[0]
    w_in = w_in_1.astype(bf)
    qkv = _matmul(xb, w_in, 0, 3 * sw, bf)
    g_c = _matmul(xb, w_in, 3 * sw, sw, jnp.float32)
    y_c = _stick_breaking(qkv.reshape(bsz, s, 3 * sw), g_c.reshape(bsz, s, sw), sw // HEAD_DIM)
    y = _matmul(y_c.reshape(bsz * s, sw), w_out_1.astype(bf), 0, d, jnp.float32)
    xf, xb = _residual_ln(xf, y, ln1_g_1, ln1_b_1)
    xf, xb = _memory_block(xf, xb, memb, w_xq_1.astype(bf), w_xkv_1.astype(bf), w_xo_1.astype(bf),
                           ln2_g_1, ln2_b_1, bsz, s)
    return xf.reshape(bsz, s, d)
```

```python
import functools
import math

import numpy as np
import jax
import jax.numpy as jnp
from jax import lax
from jax.experimental import pallas as pl
from jax.experimental.pallas import tpu as pltpu

DEPTH = 2
HEAD_DIM = 128
CONV_K = 3
MOBA_BLOCK = 256
MOBA_TOPK = 3
XATTN_HEADS = 4
REL_BUCKETS = 32
REL_MAX_DIST = 128
LN_EPS = 1e-5
NEG_INF = -1e30
DEEPNORM_ALPHA = (2 * DEPTH) ** 0.25
LOG2E = math.log2(math.e)

SB_BLOCK = 256
SB_HEADS_PER_STEP = 2
VMEM_LIMIT = 56 * 1024 * 1024

_NT = (((1,), (1,)), ((), ()))


def _cparams(n_axes):
    return pltpu.CompilerParams(dimension_semantics=("arbitrary",) * n_axes,
                                vmem_limit_bytes=VMEM_LIMIT)


def _silu(g):
    return g * jax.nn.sigmoid(g)


def _split_bf16(x):
    hi = x.astype(jnp.bfloat16)
    lo = (x - hi.astype(jnp.float32)).astype(jnp.bfloat16)
    return hi, lo


def _mm_kernel(a_ref, w_ref, o_ref, *, scaled_blocks, scale):
    acc = jnp.dot(a_ref[...], w_ref[...], preferred_element_type=jnp.float32)
    if scaled_blocks:
        acc = acc * jnp.where(pl.program_id(1) < scaled_blocks, scale, 1.0)
    o_ref[...] = acc.astype(o_ref.dtype)


def _matmul(a, w, n_start, n_size, out_dtype, bm=1024, bn=1024, scaled_cols=0, scale=1.0):
    m, k = a.shape
    bm = min(bm, m)
    bn = math.gcd(math.gcd(math.gcd(bn, n_size), n_start), scaled_cols)
    assert m % bm == 0 and bn % 128 == 0
    joff = n_start // bn
    return pl.pallas_call(
        functools.partial(_mm_kernel, scaled_blocks=scaled_cols // bn, scale=scale),
        out_shape=jax.ShapeDtypeStruct((m, n_size), out_dtype),
        grid=(m // bm, n_size // bn),
        in_specs=[pl.BlockSpec((bm, k), lambda i, j: (i, 0)),
                  pl.BlockSpec((k, bn), lambda i, j: (0, j + joff))],
        out_specs=pl.BlockSpec((bm, bn), lambda i, j: (i, j)),
        compiler_params=_cparams(2),
        name="matmul",
    )(a, w)


def _ln_kernel(x_ref, y_ref, g_ref, b_ref, of_ref, ob_ref):
    r = DEEPNORM_ALPHA * x_ref[...] + y_ref[...]
    mu = jnp.mean(r, axis=-1, keepdims=True)
    d = r - mu
    var = jnp.mean(d * d, axis=-1, keepdims=True)
    o = d * lax.rsqrt(var + LN_EPS) * g_ref[...] + b_ref[...]
    of_ref[...] = o
    ob_ref[...] = o.astype(jnp.bfloat16)


def _residual_ln(x, y, g, b, bm=256):
    m, d = x.shape
    bm = min(bm, m)
    row = pl.BlockSpec((bm, d), lambda i: (i, 0))
    vec = pl.BlockSpec((1, d), lambda i: (0, 0))
    return pl.pallas_call(
        _ln_kernel,
        out_shape=(jax.ShapeDtypeStruct((m, d), jnp.float32),
                   jax.ShapeDtypeStruct((m, d), jnp.bfloat16)),
        grid=(m // bm,),
        in_specs=[row, row, vec, vec],
        out_specs=(row, row),
        compiler_params=_cparams(1),
        name="residual_ln",
    )(x, y, g.reshape(1, d), b.reshape(1, d))


def _convgate_kernel(h_ref, b_ref, c_ref, g_ref, w_ref, o_ref):
    u = c_ref[0] * h_ref[0]
    row = lax.broadcasted_iota(jnp.int32, u.shape, 0)
    u1 = jnp.where(row >= 1, pltpu.roll(u, 1, 0), 0.0)
    u2 = jnp.where(row >= 2, pltpu.roll(u, 2, 0), 0.0)
    w = w_ref[...]
    conv = w[0:1, :] * u2 + w[1:2, :] * u1 + w[2:3, :] * u
    o_ref[0] = (b_ref[0] * conv * _silu(g_ref[0])).astype(o_ref.dtype)


def _conv_gate(p, conv_w, width, bc=256):
    bsz, s, _ = p.shape
    bc = min(bc, width)
    nc = width // bc

    def col(group):
        return pl.BlockSpec((1, s, bc), lambda b, j: (b, 0, j + group * nc))

    return pl.pallas_call(
        _convgate_kernel,
        out_shape=jax.ShapeDtypeStruct((bsz, s, width), jnp.bfloat16),
        grid=(bsz, nc),
        in_specs=[col(0), col(1), col(2), col(3),
                  pl.BlockSpec((CONV_K, bc), lambda b, j: (0, j))],
        out_specs=pl.BlockSpec((1, s, bc), lambda b, j: (b, 0, j)),
        compiler_params=_cparams(2),
        name="conv_gate",
    )(p, p, p, p, conv_w)


def _t5_bucket_thresholds():
    d = np.arange(0, 4 * REL_MAX_DIST)
    max_exact = REL_BUCKETS // 2
    nf = np.maximum(d, 1).astype(np.float32)
    large = max_exact + (np.log(nf / np.float32(max_exact)) / np.float32(math.log(REL_MAX_DIST / max_exact))
                         * np.float32(REL_BUCKETS - max_exact)).astype(np.int32)
    large = np.minimum(large, REL_BUCKETS - 1)
    bucket = np.where(d < max_exact, d, large)
    assert np.all(np.diff(bucket) >= 0) and bucket[-1] == REL_BUCKETS - 1
    return [int(np.argmax(bucket >= b)) for b in range(REL_BUCKETS)]


_BUCKET_THR = _t5_bucket_thresholds()


def _moba_kernel(rb_ref, q_ref, k_ref, v_ref, g_ref, o_ref, t0_ref, t1_ref, *, nb):
    blk = MOBA_BLOCK
    h = pl.program_id(0)
    scale = HEAD_DIM ** -0.5
    qo = lax.broadcasted_iota(jnp.int32, (blk, blk), 0)
    ko = lax.broadcasted_iota(jnp.int32, (blk, blk), 1)

    @pl.when(pl.program_id(1) == 0)
    def _():
        def table(dist):
            t = jnp.full(dist.shape, rb_ref[h, REL_BUCKETS - 1], jnp.float32)
            for bkt in range(REL_BUCKETS - 2, -1, -1):
                t = jnp.where(dist < _BUCKET_THR[bkt + 1], rb_ref[h, bkt], t)
            return t
        t0_ref[...] = table(qo - ko)
        t1_ref[...] = table(qo - ko + blk)

    rb_far = rb_ref[h, REL_BUCKETS - 1]
    assert _BUCKET_THR[REL_BUCKETS - 1] <= blk + 1

    kbar = jnp.concatenate(
        [jnp.mean(k_ref[0, j * blk:(j + 1) * blk, :].astype(jnp.float32), axis=0, keepdims=True)
         for j in range(nb)] + [jnp.zeros((16 - nb, HEAD_DIM), jnp.float32)], axis=0)
    kbar_hi, kbar_lo = _split_bf16(kbar)

    eye = jnp.where(qo == ko, 1.0, 0.0).astype(jnp.bfloat16)
    jidx = lax.broadcasted_iota(jnp.int32, (16, blk), 0)

    for i in range(nb):
        q_i = q_ref[0, i * blk:(i + 1) * blk, :]
        nk = (i + 1) * blk
        s = lax.dot_general(q_i, k_ref[0, 0:nk, :], _NT,
                            preferred_element_type=jnp.float32) * scale
        pieces = []
        if i > 0:
            if i > MOBA_TOPK:
                gate = (lax.dot_general(kbar_hi, q_i, _NT, preferred_element_type=jnp.float32)
                        + lax.dot_general(kbar_lo, q_i, _NT, preferred_element_type=jnp.float32))
                cnt = jnp.zeros((16, blk), jnp.float32)
                for jp in range(i):
                    row = gate[jp:jp + 1, :]
                    beats = jnp.where(row > gate, 1.0,
                                      jnp.where(row == gate, jnp.where(jidx > jp, 1.0, 0.0), 0.0))
                    cnt = cnt + beats
                sel_t = jnp.where(cnt < MOBA_TOPK, jnp.where(jidx < i, 1.0, 0.0), 0.0)
                sel_t = jnp.concatenate([sel_t, jnp.zeros((128 - 16, blk), jnp.float32)], axis=0)
                sel = lax.dot_general(eye, sel_t.astype(jnp.bfloat16), _NT,
                                      preferred_element_type=jnp.float32)
                er = lax.broadcasted_iota(jnp.int32, (128, i * blk), 0)
                ec = lax.broadcasted_iota(jnp.int32, (128, i * blk), 1)
                expand = jnp.where((ec >= er * blk) & (ec < er * blk + blk), 1.0, 0.0)
                keep = jnp.dot(sel.astype(jnp.bfloat16), expand.astype(jnp.bfloat16),
                               preferred_element_type=jnp.float32) > 0.5
            for j in range(i):
                sj = s[:, j * blk:(j + 1) * blk]
                lg = sj + (t1_ref[...] if j == i - 1 else rb_far)
                if i > MOBA_TOPK:
                    lg = jnp.where(keep[:, j * blk:(j + 1) * blk], lg, NEG_INF)
                pieces.append(lg)
        pieces.append(jnp.where(ko <= qo, s[:, i * blk:nk] + t0_ref[...], NEG_INF))
        lg = pieces[0] if len(pieces) == 1 else jnp.concatenate(pieces, axis=-1)
        m = jnp.max(lg, axis=-1, keepdims=True)
        p = jnp.exp(lg - m)
        l = jnp.sum(p, axis=-1, keepdims=True)
        o = jnp.dot(p.astype(jnp.bfloat16), v_ref[0, 0:nk, :],
                    preferred_element_type=jnp.float32) / l
        o_ref[0, i * blk:(i + 1) * blk, :] = (o * _silu(g_ref[0, i * blk:(i + 1) * blk, :])
                                              ).astype(o_ref.dtype)


def _moba(qkv, g, rel_bias_t, heads):
    bsz, s, _ = qkv.shape
    assert s % MOBA_BLOCK == 0
    nb = s // MOBA_BLOCK
    assert MOBA_TOPK < nb <= 16

    def col(group):
        return pl.BlockSpec((1, s, HEAD_DIM), lambda h, b: (b, 0, h + group * heads))

    return pl.pallas_call(
        functools.partial(_moba_kernel, nb=nb),
        out_shape=jax.ShapeDtypeStruct((bsz, s, heads * HEAD_DIM), jnp.bfloat16),
        grid=(heads, bsz),
        in_specs=[pl.BlockSpec(memory_space=pltpu.SMEM), col(0), col(1), col(2),
                  pl.BlockSpec((1, s, HEAD_DIM), lambda h, b: (b, 0, h))],
        out_specs=pl.BlockSpec((1, s, HEAD_DIM), lambda h, b: (b, 0, h)),
        scratch_shapes=[pltpu.VMEM((MOBA_BLOCK, MOBA_BLOCK), jnp.float32),
                        pltpu.VMEM((MOBA_BLOCK, MOBA_BLOCK), jnp.float32)],
        compiler_params=_cparams(2),
        name="moba",
    )(rel_bias_t, qkv, qkv, qkv, g)


def _sb_kernel(q_ref, k_ref, v_ref, g_ref, o_ref, *, nq):
    t = SB_BLOCK
    tq = 2 * t
    tri = jnp.where(lax.broadcasted_iota(jnp.int32, (t, t), 0)
                    > lax.broadcasted_iota(jnp.int32, (t, t), 1), 1.0, 0.0).astype(jnp.bfloat16)
    tri2 = jnp.concatenate([tri, tri], axis=0)

    def tiles(qs, j0, state, diag):
        heads = range(len(qs))
        cols = [slice(hh * HEAD_DIM, (hh + 1) * HEAD_DIM) for hh in heads]
        zs = [lax.dot_general(qs[hh], k_ref[0, pl.ds(j0, t), cols[hh]], _NT,
                              preferred_element_type=jnp.float32) for hh in heads]
        if diag:
            strict = (lax.broadcasted_iota(jnp.int32, zs[0].shape, 1)
                      < lax.broadcasted_iota(jnp.int32, zs[0].shape, 0))
            zs = [jnp.where(strict, z, NEG_INF) for z in zs]
        sps = [jnp.where(z > 30.0, z, jnp.log(1.0 + jnp.exp2(z)) * LOG2E) for z in zs]
        rems = [jnp.dot(jnp.concatenate(_split_bf16(sps[hh]), axis=-1), tri2,
                        preferred_element_type=jnp.float32) + state[2 * hh] for hh in heads]
        ws = [jnp.exp2((zs[hh] - sps[hh]) - rems[hh]).astype(jnp.bfloat16) for hh in heads]
        out = []
        for hh in heads:
            acc = state[2 * hh + 1] + jnp.dot(ws[hh], v_ref[0, pl.ds(j0, t), cols[hh]],
                                              preferred_element_type=jnp.float32)
            out.extend((rems[hh][:, 0:1] + sps[hh][:, 0:1], acc))
        return tuple(out)

    def q_block(i, _):
        q0 = pl.multiple_of(i * tq, tq)
        q1 = pl.multiple_of(i * tq + t, t)
        heads = range(SB_HEADS_PER_STEP)
        cols = [slice(hh * HEAD_DIM, (hh + 1) * HEAD_DIM) for hh in heads]
        zero_c = jnp.zeros((t, 1), jnp.float32)
        zero_a = jnp.zeros((t, HEAD_DIM), jnp.float32)
        low = tiles([q_ref[0, pl.ds(q1, t), c] for c in cols], q1, (zero_c, zero_a) * len(cols), True)
        q_i = [q_ref[0, pl.ds(q0, tq), c] for c in cols]
        state = []
        for hh in heads:
            state.append(jnp.concatenate([zero_c, low[2 * hh]], axis=0))
            state.append(jnp.concatenate([zero_a, low[2 * hh + 1]], axis=0))
        state = tiles(q_i, q0, tuple(state), True)

        def past(n, state):
            return tiles(q_i, pl.multiple_of((2 * i - 1 - n) * t, t), state, False)

        state = lax.fori_loop(0, 2 * i, past, state)
        acc = jnp.concatenate([state[2 * hh + 1] for hh in heads], axis=-1)
        o_ref[0, pl.ds(q0, tq), :] = (acc * _silu(g_ref[0, pl.ds(q0, tq), :])).astype(o_ref.dtype)
        return 0

    lax.fori_loop(0, nq, q_block, 0)


def _stick_breaking(qkv, g, heads):
    bsz, s, _ = qkv.shape
    assert s % (2 * SB_BLOCK) == 0 and heads % SB_HEADS_PER_STEP == 0
    steps = heads // SB_HEADS_PER_STEP
    width = SB_HEADS_PER_STEP * HEAD_DIM

    def col(group):
        return pl.BlockSpec((1, s, width), lambda h, b: (b, 0, h + group * steps))

    return pl.pallas_call(
        functools.partial(_sb_kernel, nq=s // (2 * SB_BLOCK)),
        out_shape=jax.ShapeDtypeStruct((bsz, s, heads * HEAD_DIM), jnp.bfloat16),
        grid=(steps, bsz),
        in_specs=[col(0), col(1), col(2), col(0)],
        out_specs=col(0),
        compiler_params=_cparams(2),
        name="stick_breaking",
    )(qkv, qkv, qkv, g)


def _xattn_kernel(q_ref, g_ref, k_ref, v_ref, o_ref, *, dh):
    scale = dh ** -0.5
    for h in range(XATTN_HEADS):
        cols = slice(h * dh, (h + 1) * dh)
        s = lax.dot_general(q_ref[0, :, cols], k_ref[0, :, cols], _NT,
                            preferred_element_type=jnp.float32) * scale
        m = jnp.max(s, axis=-1, keepdims=True)
        p = jnp.exp(s - m)
        l = jnp.sum(p, axis=-1, keepdims=True)
        o = jnp.dot(p.astype(jnp.bfloat16), v_ref[0, :, cols],
                    preferred_element_type=jnp.float32) / l
        o_ref[0, :, cols] = (o * _silu(g_ref[0, :, cols])).astype(o_ref.dtype)


def _xattn(q, g, kv, bm=512):
    bsz, s, w = q.shape
    n_mem = kv.shape[1]
    bm = min(bm, s)
    row = pl.BlockSpec((1, bm, w), lambda b, i: (b, i, 0))
    return pl.pallas_call(
        functools.partial(_xattn_kernel, dh=w // XATTN_HEADS),
        out_shape=jax.ShapeDtypeStruct((bsz, s, w), jnp.bfloat16),
        grid=(bsz, s // bm),
        in_specs=[row, row,
                  pl.BlockSpec((1, n_mem, w), lambda b, i: (b, 0, 0)),
                  pl.BlockSpec((1, n_mem, w), lambda b, i: (b, 0, 1))],
        out_specs=row,
        compiler_params=_cparams(2),
        name="xattn",
    )(q, g, kv, kv)


def _memory_block(x_f32, x_bf16, mem_bf16, w_xq, w_xkv, w_xo, g2, b2, bsz, s):
    d = x_f32.shape[-1]
    xw = w_xo.shape[0]
    q = _matmul(x_bf16, w_xq, 0, xw, jnp.bfloat16)
    g = _matmul(x_bf16, w_xq, xw, xw, jnp.float32)
    kv = _matmul(mem_bf16, w_xkv, 0, 2 * xw, jnp.bfloat16)
    n_mem = mem_bf16.shape[0] // bsz
    o = _xattn(q.reshape(bsz, s, xw), g.reshape(bsz, s, xw), kv.reshape(bsz, n_mem, 2 * xw))
    y = _matmul(o.reshape(bsz * s, xw), w_xo, 0, d, jnp.float32)
    return _residual_ln(x_f32, y, g2, b2)


def kernel(x, mem, rel_bias, w_in_0, conv_w_0, w_out_0, ln1_g_0, ln1_b_0, w_xq_0, w_xkv_0, w_xo_0, ln2_g_0, ln2_b_0, w_in_1, w_out_1, ln1_g_1, ln1_b_1, w_xq_1, w_xkv_1, w_xo_1, ln2_g_1, ln2_b_1):
    bsz, s, d = x.shape
    bf = jnp.bfloat16
    xf = x.reshape(bsz * s, d)
    xb = xf.astype(bf)
    memb = mem.reshape(-1, d).astype(bf)

    cw = conv_w_0.shape[1]
    mw = w_in_0.shape[1] // 4 - cw
    heads = mw // HEAD_DIM
    w_in = w_in_0.astype(bf)
    p_conv = _matmul(xb, w_in, 0, 4 * cw, jnp.float32)
    qkv = _matmul(xb, w_in, 4 * cw, 3 * mw, bf)
    g_b = _matmul(xb, w_in, 4 * cw + 3 * mw, mw, jnp.float32)
    y_a = _conv_gate(p_conv.reshape(bsz, s, 4 * cw), conv_w_0, cw)
    y_b = _moba(qkv.reshape(bsz, s, 3 * mw), g_b.reshape(bsz, s, mw), rel_bias.T, heads)
    y_cat = jnp.concatenate([y_a, y_b], axis=-1).reshape(bsz * s, cw + mw)
    y = _matmul(y_cat, w_out_0.astype(bf), 0, d, jnp.float32)
    xf, xb = _residual_ln(xf, y, ln1_g_0, ln1_b_0)
    xf, xb = _memory_block(xf, xb, memb, w_xq_0.astype(bf), w_xkv_0.astype(bf), w_xo_0.astype(bf),
                           ln2_g_0, ln2_b_0, bsz, s)

    sw = w_out_1.shape[0]
    w_in = w_in_1.astype(bf)
    qkv = _matmul(xb, w_in, 0, 3 * sw, bf, scaled_cols=sw, scale=HEAD_DIM ** -0.5 * LOG2E)
    g_c = _matmul(xb, w_in, 3 * sw, sw, jnp.float32)
    y_c = _stick_breaking(qkv.reshape(bsz, s, 3 * sw), g_c.reshape(bsz, s, sw), sw // HEAD_DIM)
    y = _matmul(y_c.reshape(bsz * s, sw), w_out_1.astype(bf), 0, d, jnp.float32)
    xf, xb = _residual_ln(xf, y, ln1_g_1, ln1_b_1)
    xf, xb = _memory_block(xf, xb, memb, w_xq_1.astype(bf), w_xkv_1.astype(bf), w_xo_1.astype(bf),
                           ln2_g_1, ln2_b_1, bsz, s)
    return xf.reshape(bsz, s, d)
```

```python
import functools
import math

import numpy as np
import jax
import jax.numpy as jnp
from jax import lax
from jax.experimental import pallas as pl
from jax.experimental.pallas import tpu as pltpu

DEPTH = 2
HEAD_DIM = 128
CONV_K = 3
MOBA_BLOCK = 256
MOBA_TOPK = 3
XATTN_HEADS = 4
REL_BUCKETS = 32
REL_MAX_DIST = 128
LN_EPS = 1e-5
NEG_INF = -1e30
DEEPNORM_ALPHA = (2 * DEPTH) ** 0.25
LOG2E = math.log2(math.e)

SB_BLOCK = 256
SB_HEADS_PER_STEP = 2
VMEM_LIMIT = 56 * 1024 * 1024

_NT = (((1,), (1,)), ((), ()))


def _cparams(n_axes):
    return pltpu.CompilerParams(dimension_semantics=("arbitrary",) * n_axes,
                                vmem_limit_bytes=VMEM_LIMIT)


def _silu(g):
    return g * jax.nn.sigmoid(g)


def _split_bf16(x):
    hi = x.astype(jnp.bfloat16)
    lo = (x - hi.astype(jnp.float32)).astype(jnp.bfloat16)
    return hi, lo


def _mm_kernel(a_ref, w_ref, o_ref, *, scaled_blocks, scale):
    acc = jnp.dot(a_ref[...], w_ref[...], preferred_element_type=jnp.float32)
    if scaled_blocks:
        acc = acc * jnp.where(pl.program_id(1) < scaled_blocks, scale, 1.0)
    o_ref[...] = acc.astype(o_ref.dtype)


def _matmul(a, w, n_start, n_size, out_dtype, bm=1024, bn=1024, scaled_cols=0, scale=1.0):
    m, k = a.shape
    bm = min(bm, m)
    bn = math.gcd(math.gcd(math.gcd(bn, n_size), n_start), scaled_cols)
    assert m % bm == 0 and bn % 128 == 0
    joff = n_start // bn
    return pl.pallas_call(
        functools.partial(_mm_kernel, scaled_blocks=scaled_cols // bn, scale=scale),
        out_shape=jax.ShapeDtypeStruct((m, n_size), out_dtype),
        grid=(m // bm, n_size // bn),
        in_specs=[pl.BlockSpec((bm, k), lambda i, j: (i, 0)),
                  pl.BlockSpec((k, bn), lambda i, j: (0, j + joff))],
        out_specs=pl.BlockSpec((bm, bn), lambda i, j: (i, j)),
        compiler_params=_cparams(2),
        name="matmul",
    )(a, w)


def _mm2_kernel(a1_ref, a2_ref, w_ref, o_ref):
    k1 = a1_ref.shape[1]
    acc = jnp.dot(a1_ref[...], w_ref[0:k1, :], preferred_element_type=jnp.float32)
    acc = acc + jnp.dot(a2_ref[...], w_ref[k1:, :], preferred_element_type=jnp.float32)
    o_ref[...] = acc.astype(o_ref.dtype)


def _matmul_cat(a1, a2, w, out_dtype, bm=1024, bn=1024):
    m, k1 = a1.shape
    k2 = a2.shape[1]
    n = w.shape[1]
    bm = min(bm, m)
    bn = min(bn, n)
    assert m % bm == 0 and n % bn == 0 and w.shape[0] == k1 + k2
    return pl.pallas_call(
        _mm2_kernel,
        out_shape=jax.ShapeDtypeStruct((m, n), out_dtype),
        grid=(m // bm, n // bn),
        in_specs=[pl.BlockSpec((bm, k1), lambda i, j: (i, 0)),
                  pl.BlockSpec((bm, k2), lambda i, j: (i, 0)),
                  pl.BlockSpec((k1 + k2, bn), lambda i, j: (0, j))],
        out_specs=pl.BlockSpec((bm, bn), lambda i, j: (i, j)),
        compiler_params=_cparams(2),
        name="matmul_cat",
    )(a1, a2, w)


def _layer_norm_rows(r, g, b):
    mu = jnp.mean(r, axis=-1, keepdims=True)
    d = r - mu
    var = jnp.mean(d * d, axis=-1, keepdims=True)
    return d * lax.rsqrt(var + LN_EPS) * g + b


def _ln_kernel(x_ref, y_ref, g_ref, b_ref, of_ref, ob_ref):
    o = _layer_norm_rows(DEEPNORM_ALPHA * x_ref[...] + y_ref[...], g_ref[...], b_ref[...])
    of_ref[...] = o
    ob_ref[...] = o.astype(jnp.bfloat16)


def _mm_ln_kernel(a_ref, w_ref, x_ref, g_ref, b_ref, of_ref, *ob_ref):
    y = jnp.dot(a_ref[...], w_ref[...], preferred_element_type=jnp.float32)
    o = _layer_norm_rows(DEEPNORM_ALPHA * x_ref[...] + y, g_ref[...], b_ref[...])
    of_ref[...] = o
    if ob_ref:
        ob_ref[0][...] = o.astype(jnp.bfloat16)


def _matmul_residual_ln(a, w, x, g, b, emit_bf16, bm=256):
    m, k = a.shape
    d = w.shape[1]
    bm = min(bm, m)
    row = pl.BlockSpec((bm, d), lambda i: (i, 0))
    vec = pl.BlockSpec((1, d), lambda i: (0, 0))
    out_shape = [jax.ShapeDtypeStruct((m, d), jnp.float32)]
    if emit_bf16:
        out_shape.append(jax.ShapeDtypeStruct((m, d), jnp.bfloat16))
    out = pl.pallas_call(
        _mm_ln_kernel,
        out_shape=tuple(out_shape),
        grid=(m // bm,),
        in_specs=[pl.BlockSpec((bm, k), lambda i: (i, 0)),
                  pl.BlockSpec((k, d), lambda i: (0, 0)), row, vec, vec],
        out_specs=tuple([row] * len(out_shape)),
        compiler_params=_cparams(1),
        name="matmul_residual_ln",
    )(a, w, x, g.reshape(1, d), b.reshape(1, d))
    return out if emit_bf16 else (out[0], None)


def _residual_ln(x, y, g, b, bm=256):
    m, d = x.shape
    bm = min(bm, m)
    row = pl.BlockSpec((bm, d), lambda i: (i, 0))
    vec = pl.BlockSpec((1, d), lambda i: (0, 0))
    return pl.pallas_call(
        _ln_kernel,
        out_shape=(jax.ShapeDtypeStruct((m, d), jnp.float32),
                   jax.ShapeDtypeStruct((m, d), jnp.bfloat16)),
        grid=(m // bm,),
        in_specs=[row, row, vec, vec],
        out_specs=(row, row),
        compiler_params=_cparams(1),
        name="residual_ln",
    )(x, y, g.reshape(1, d), b.reshape(1, d))


def _conv_proj_kernel(a_ref, wh_ref, wb_ref, wc_ref, wg_ref, cw_ref, o_ref, tail_ref, *,
                      tiles_per_seq):
    i = pl.program_id(0)
    j = pl.program_id(1)
    a = a_ref[...]
    u = (jnp.dot(a, wc_ref[...], preferred_element_type=jnp.float32)
         * jnp.dot(a, wh_ref[...], preferred_element_type=jnp.float32))
    bm = u.shape[0]

    @pl.when(i % tiles_per_seq == 0)
    def _():
        tail_ref[j] = jnp.zeros(tail_ref.shape[1:], jnp.float32)

    prev = tail_ref[j]
    tail_ref[j] = u[bm - 8:bm, :]
    row = lax.broadcasted_iota(jnp.int32, u.shape, 0)
    u1 = jnp.where(row == 0, prev[7:8, :], pltpu.roll(u, 1, 0))
    u2 = jnp.where(row == 0, prev[6:7, :],
                   jnp.where(row == 1, prev[7:8, :], pltpu.roll(u, 2, 0)))
    w = cw_ref[...]
    conv = w[0:1, :] * u2 + w[1:2, :] * u1 + w[2:3, :] * u
    b = jnp.dot(a, wb_ref[...], preferred_element_type=jnp.float32)
    g = jnp.dot(a, wg_ref[...], preferred_element_type=jnp.float32)
    o_ref[...] = (b * conv * _silu(g)).astype(o_ref.dtype)


def _conv_proj(a, w_in, conv_w, width, seq_len, bm=1024, bc=256):
    m, k = a.shape
    bm = min(bm, seq_len)
    bc = min(bc, width)
    nc = width // bc
    assert seq_len % bm == 0 and m % seq_len == 0 and width % bc == 0 and bm % 8 == 0

    def wcol(group):
        return pl.BlockSpec((k, bc), lambda i, j: (0, j + group * nc))

    return pl.pallas_call(
        functools.partial(_conv_proj_kernel, tiles_per_seq=seq_len // bm),
        out_shape=jax.ShapeDtypeStruct((m, width), jnp.bfloat16),
        grid=(m // bm, nc),
        in_specs=[pl.BlockSpec((bm, k), lambda i, j: (i, 0)),
                  wcol(0), wcol(1), wcol(2), wcol(3),
                  pl.BlockSpec((CONV_K, bc), lambda i, j: (0, j))],
        out_specs=pl.BlockSpec((bm, bc), lambda i, j: (i, j)),
        scratch_shapes=[pltpu.VMEM((nc, 8, bc), jnp.float32)],
        compiler_params=_cparams(2),
        name="conv_proj",
    )(a, w_in, w_in, w_in, w_in, conv_w)


def _t5_bucket_thresholds():
    d = np.arange(0, 4 * REL_MAX_DIST)
    max_exact = REL_BUCKETS // 2
    nf = np.maximum(d, 1).astype(np.float32)
    large = max_exact + (np.log(nf / np.float32(max_exact)) / np.float32(math.log(REL_MAX_DIST / max_exact))
                         * np.float32(REL_BUCKETS - max_exact)).astype(np.int32)
    large = np.minimum(large, REL_BUCKETS - 1)
    bucket = np.where(d < max_exact, d, large)
    assert np.all(np.diff(bucket) >= 0) and bucket[-1] == REL_BUCKETS - 1
    return [int(np.argmax(bucket >= b)) for b in range(REL_BUCKETS)]


_BUCKET_THR = _t5_bucket_thresholds()
_MOBA_BLOCK_SHIFT = MOBA_BLOCK.bit_length() - 1
assert 1 << _MOBA_BLOCK_SHIFT == MOBA_BLOCK


def _moba_kernel(rb_ref, q_ref, k_ref, v_ref, g_ref, o_ref, t0_ref, t1_ref, kaug_ref, *, nb):
    blk = MOBA_BLOCK
    h = pl.program_id(0)
    s_len = nb * blk
    qo = lax.broadcasted_iota(jnp.int32, (blk, blk), 0)
    ko = lax.broadcasted_iota(jnp.int32, (blk, blk), 1)

    @pl.when(pl.program_id(1) == 0)
    def _():
        def table(dist):
            t = jnp.full(dist.shape, rb_ref[h, REL_BUCKETS - 1], jnp.float32)
            for bkt in range(REL_BUCKETS - 2, -1, -1):
                t = jnp.where(dist < _BUCKET_THR[bkt + 1], rb_ref[h, bkt], t)
            return t * LOG2E
        t0_ref[...] = table(qo - ko)
        t1_ref[...] = table(qo - ko + blk)

    rb_far = rb_ref[h, REL_BUCKETS - 1] * LOG2E
    assert _BUCKET_THR[REL_BUCKETS - 1] <= blk + 1

    kaug_ref[:, 0:HEAD_DIM] = k_ref[0]
    kaug_ref[:, HEAD_DIM:2 * HEAD_DIM] = jnp.where(
        jnp.right_shift(lax.broadcasted_iota(jnp.int32, (s_len, HEAD_DIM), 0), _MOBA_BLOCK_SHIFT)
        == lax.broadcasted_iota(jnp.int32, (s_len, HEAD_DIM), 1), 1.0, 0.0).astype(jnp.bfloat16)

    kbar = jnp.concatenate(
        [jnp.mean(k_ref[0, j * blk:(j + 1) * blk, :].astype(jnp.float32), axis=0, keepdims=True)
         for j in range(nb)] + [jnp.zeros((16 - nb, HEAD_DIM), jnp.float32)], axis=0)
    kbar_hi, kbar_lo = _split_bf16(kbar)

    eye = jnp.where(qo == ko, 1.0, 0.0).astype(jnp.bfloat16)
    jidx = lax.broadcasted_iota(jnp.int32, (16, blk), 0)

    def logits(i):
        q_i = q_ref[0, i * blk:(i + 1) * blk, :]
        nk = (i + 1) * blk
        if i > MOBA_TOPK:
            gate = (lax.dot_general(kbar_hi, q_i, _NT, preferred_element_type=jnp.float32)
                    + lax.dot_general(kbar_lo, q_i, _NT, preferred_element_type=jnp.float32))
            cnt = jnp.zeros((16, blk), jnp.float32)
            for jp in range(i):
                row = gate[jp:jp + 1, :]
                beats = jnp.where(row > gate, 1.0,
                                  jnp.where(row == gate, jnp.where(jidx > jp, 1.0, 0.0), 0.0))
                cnt = cnt + beats
            drop_t = jnp.where(cnt < MOBA_TOPK, 0.0, jnp.where(jidx < i, NEG_INF, 0.0))
            drop_t = jnp.concatenate([drop_t, jnp.zeros((128 - 16, blk), jnp.float32)], axis=0)
            drop = lax.dot_general(eye, drop_t.astype(jnp.bfloat16), _NT,
                                   preferred_element_type=jnp.float32)
            q_aug = jnp.concatenate([q_i, drop.astype(jnp.bfloat16)], axis=-1)
            return lax.dot_general(q_aug, kaug_ref[0:nk, :], _NT,
                                   preferred_element_type=jnp.float32)
        return lax.dot_general(q_i, k_ref[0, 0:nk, :], _NT, preferred_element_type=jnp.float32)

    def attend(i, s):
        nk = (i + 1) * blk
        n_far = max(i - 1, 0) * blk
        lg_own = jnp.where(ko <= qo, s[:, i * blk:nk] + t0_ref[...], NEG_INF)
        m = jnp.max(lg_own, axis=-1, keepdims=True)
        if i > 0:
            lg_prev = s[:, (i - 1) * blk:i * blk] + t1_ref[...]
            m = jnp.maximum(m, jnp.max(lg_prev, axis=-1, keepdims=True))
        if n_far:
            m = jnp.maximum(m, jnp.max(s[:, 0:n_far], axis=-1, keepdims=True) + rb_far)
        pieces = [jnp.exp2(lg_own - m)]
        if i > 0:
            pieces.insert(0, jnp.exp2(lg_prev - m))
        if n_far:
            pieces.insert(0, jnp.exp2(s[:, 0:n_far] - (m - rb_far)))
        l = sum(jnp.sum(piece, axis=-1, keepdims=True) for piece in pieces)
        p = pieces[0] if len(pieces) == 1 else jnp.concatenate(pieces, axis=-1)
        o = jnp.dot(p.astype(jnp.bfloat16), v_ref[0, 0:nk, :],
                    preferred_element_type=jnp.float32) / l
        o_ref[0, i * blk:(i + 1) * blk, :] = (o * _silu(g_ref[0, i * blk:(i + 1) * blk, :])
                                              ).astype(o_ref.dtype)

    s_next = logits(0)
    for i in range(nb):
        s_cur, s_next = s_next, (logits(i + 1) if i + 1 < nb else None)
        attend(i, s_cur)


def _moba(qkv, g, rel_bias_t, heads):
    bsz, s, _ = qkv.shape
    assert s % MOBA_BLOCK == 0
    nb = s // MOBA_BLOCK
    assert MOBA_TOPK < nb <= 16

    def col(group):
        return pl.BlockSpec((1, s, HEAD_DIM), lambda h, b: (b, 0, h + group * heads))

    return pl.pallas_call(
        functools.partial(_moba_kernel, nb=nb),
        out_shape=jax.ShapeDtypeStruct((bsz, s, heads * HEAD_DIM), jnp.bfloat16),
        grid=(heads, bsz),
        in_specs=[pl.BlockSpec(memory_space=pltpu.SMEM), col(0), col(1), col(2),
                  pl.BlockSpec((1, s, HEAD_DIM), lambda h, b: (b, 0, h))],
        out_specs=pl.BlockSpec((1, s, HEAD_DIM), lambda h, b: (b, 0, h)),
        scratch_shapes=[pltpu.VMEM((MOBA_BLOCK, MOBA_BLOCK), jnp.float32),
                        pltpu.VMEM((MOBA_BLOCK, MOBA_BLOCK), jnp.float32),
                        pltpu.VMEM((s, 2 * HEAD_DIM), jnp.bfloat16)],
        compiler_params=_cparams(2),
        name="moba",
    )(rel_bias_t, qkv, qkv, qkv, g)


def _sb_kernel(q_ref, k_ref, v_ref, g_ref, o_ref, *, nq):
    t = SB_BLOCK
    tq = 2 * t
    tri = jnp.where(lax.broadcasted_iota(jnp.int32, (t, t), 0)
                    > lax.broadcasted_iota(jnp.int32, (t, t), 1), 1.0, 0.0).astype(jnp.bfloat16)
    tri2 = jnp.concatenate([tri, tri], axis=0)

    def tiles(qs, j0, state, diag):
        heads = range(len(qs))
        cols = [slice(hh * HEAD_DIM, (hh + 1) * HEAD_DIM) for hh in heads]
        zs = [lax.dot_general(qs[hh], k_ref[0, pl.ds(j0, t), cols[hh]], _NT,
                              preferred_element_type=jnp.float32) for hh in heads]
        if diag:
            strict = (lax.broadcasted_iota(jnp.int32, zs[0].shape, 1)
                      < lax.broadcasted_iota(jnp.int32, zs[0].shape, 0))
            zs = [jnp.where(strict, z, NEG_INF) for z in zs]
        sps = [jnp.where(z > 30.0, z, jnp.log(1.0 + jnp.exp2(z)) * LOG2E) for z in zs]
        rems = [jnp.dot(jnp.concatenate(_split_bf16(sps[hh]), axis=-1), tri2,
                        preferred_element_type=jnp.float32) + state[2 * hh] for hh in heads]
        ws = [jnp.exp2((zs[hh] - sps[hh]) - rems[hh]).astype(jnp.bfloat16) for hh in heads]
        out = []
        for hh in heads:
            acc = state[2 * hh + 1] + jnp.dot(ws[hh], v_ref[0, pl.ds(j0, t), cols[hh]],
                                              preferred_element_type=jnp.float32)
            out.extend((rems[hh][:, 0:1] + sps[hh][:, 0:1], acc))
        return tuple(out)

    def q_block(i, _):
        q0 = pl.multiple_of(i * tq, tq)
        q1 = pl.multiple_of(i * tq + t, t)
        heads = range(SB_HEADS_PER_STEP)
        cols = [slice(hh * HEAD_DIM, (hh + 1) * HEAD_DIM) for hh in heads]
        zero_c = jnp.zeros((t, 1), jnp.float32)
        zero_a = jnp.zeros((t, HEAD_DIM), jnp.float32)
        low = tiles([q_ref[0, pl.ds(q1, t), c] for c in cols], q1, (zero_c, zero_a) * len(cols), True)
        q_i = [q_ref[0, pl.ds(q0, tq), c] for c in cols]
        state = []
        for hh in heads:
            state.append(jnp.concatenate([zero_c, low[2 * hh]], axis=0))
            state.append(jnp.concatenate([zero_a, low[2 * hh + 1]], axis=0))
        state = tiles(q_i, q0, tuple(state), True)

        def past(n, state):
            return tiles(q_i, pl.multiple_of((2 * i - 1 - n) * t, t), state, False)

        state = lax.fori_loop(0, 2 * i, past, state)
        acc = jnp.concatenate([state[2 * hh + 1] for hh in heads], axis=-1)
        o_ref[0, pl.ds(q0, tq), :] = (acc * _silu(g_ref[0, pl.ds(q0, tq), :])).astype(o_ref.dtype)
        return 0

    lax.fori_loop(0, nq, q_block, 0)


def _stick_breaking(qkv, g, heads):
    bsz, s, _ = qkv.shape
    assert s % (2 * SB_BLOCK) == 0 and heads % SB_HEADS_PER_STEP == 0
    steps = heads // SB_HEADS_PER_STEP
    width = SB_HEADS_PER_STEP * HEAD_DIM

    def col(group):
        return pl.BlockSpec((1, s, width), lambda h, b: (b, 0, h + group * steps))

    return pl.pallas_call(
        functools.partial(_sb_kernel, nq=s // (2 * SB_BLOCK)),
        out_shape=jax.ShapeDtypeStruct((bsz, s, heads * HEAD_DIM), jnp.bfloat16),
        grid=(steps, bsz),
        in_specs=[col(0), col(1), col(2), col(0)],
        out_specs=col(0),
        compiler_params=_cparams(2),
        name="stick_breaking",
    )(qkv, qkv, qkv, g)


def _xattn_kernel(q_ref, g_ref, k_ref, v_ref, o_ref, *, dh):
    scale = dh ** -0.5
    for h in range(XATTN_HEADS):
        cols = slice(h * dh, (h + 1) * dh)
        s = lax.dot_general(q_ref[0, :, cols], k_ref[0, :, cols], _NT,
                            preferred_element_type=jnp.float32) * scale
        m = jnp.max(s, axis=-1, keepdims=True)
        p = jnp.exp(s - m)
        l = jnp.sum(p, axis=-1, keepdims=True)
        o = jnp.dot(p.astype(jnp.bfloat16), v_ref[0, :, cols],
                    preferred_element_type=jnp.float32) / l
        o_ref[0, :, cols] = (o * _silu(g_ref[0, :, cols])).astype(o_ref.dtype)


def _xattn(q, g, kv, bm=512):
    bsz, s, w = q.shape
    n_mem = kv.shape[1]
    bm = min(bm, s)
    row = pl.BlockSpec((1, bm, w), lambda b, i: (b, i, 0))
    return pl.pallas_call(
        functools.partial(_xattn_kernel, dh=w // XATTN_HEADS),
        out_shape=jax.ShapeDtypeStruct((bsz, s, w), jnp.bfloat16),
        grid=(bsz, s // bm),
        in_specs=[row, row,
                  pl.BlockSpec((1, n_mem, w), lambda b, i: (b, 0, 0)),
                  pl.BlockSpec((1, n_mem, w), lambda b, i: (b, 0, 1))],
        out_specs=row,
        compiler_params=_cparams(2),
        name="xattn",
    )(q, g, kv, kv)


def _memory_block(x_f32, x_bf16, mem_bf16, w_xq, w_xkv, w_xo, g2, b2, bsz, s, emit_bf16):
    xw = w_xo.shape[0]
    q = _matmul(x_bf16, w_xq, 0, xw, jnp.bfloat16)
    g = _matmul(x_bf16, w_xq, xw, xw, jnp.float32)
    kv = _matmul(mem_bf16, w_xkv, 0, 2 * xw, jnp.bfloat16)
    n_mem = mem_bf16.shape[0] // bsz
    o = _xattn(q.reshape(bsz, s, xw), g.reshape(bsz, s, xw), kv.reshape(bsz, n_mem, 2 * xw))
    return _matmul_residual_ln(o.reshape(bsz * s, xw), w_xo, x_f32, g2, b2, emit_bf16)


def kernel(x, mem, rel_bias, w_in_0, conv_w_0, w_out_0, ln1_g_0, ln1_b_0, w_xq_0, w_xkv_0, w_xo_0, ln2_g_0, ln2_b_0, w_in_1, w_out_1, ln1_g_1, ln1_b_1, w_xq_1, w_xkv_1, w_xo_1, ln2_g_1, ln2_b_1):
    bsz, s, d = x.shape
    bf = jnp.bfloat16
    xf = x.reshape(bsz * s, d)
    xb = xf.astype(bf)
    memb = mem.reshape(-1, d).astype(bf)

    cw = conv_w_0.shape[1]
    mw = w_in_0.shape[1] // 4 - cw
    heads = mw // HEAD_DIM
    w_in = w_in_0.astype(bf)
    y_a = _conv_proj(xb, w_in, conv_w_0, cw, s)
    qkv = _matmul(xb, w_in, 4 * cw, 3 * mw, bf, scaled_cols=mw, scale=HEAD_DIM ** -0.5 * LOG2E)
    g_b = _matmul(xb, w_in, 4 * cw + 3 * mw, mw, jnp.float32)
    y_b = _moba(qkv.reshape(bsz, s, 3 * mw), g_b.reshape(bsz, s, mw), rel_bias.T, heads)
    y = _matmul_cat(y_a, y_b.reshape(bsz * s, mw), w_out_0.astype(bf),
                    jnp.float32)
    xf, xb = _residual_ln(xf, y, ln1_g_0, ln1_b_0)
    xf, xb = _memory_block(xf, xb, memb, w_xq_0.astype(bf), w_xkv_0.astype(bf), w_xo_0.astype(bf),
                           ln2_g_0, ln2_b_0, bsz, s, True)

    sw = w_out_1.shape[0]
    w_in = w_in_1.astype(bf)
    qkv = _matmul(xb, w_in, 0, 3 * sw, bf, scaled_cols=sw, scale=HEAD_DIM ** -0.5 * LOG2E)
    g_c = _matmul(xb, w_in, 3 * sw, sw, jnp.float32)
    y_c = _stick_breaking(qkv.reshape(bsz, s, 3 * sw), g_c.reshape(bsz, s, sw), sw // HEAD_DIM)
    y = _matmul(y_c.reshape(bsz * s, sw), w_out_1.astype(bf), 0, d, jnp.float32)
    xf, xb = _residual_ln(xf, y, ln1_g_1, ln1_b_1)
    xf, _ = _memory_block(xf, xb, memb, w_xq_1.astype(bf), w_xkv_1.astype(bf), w_xo_1.astype(bf),
                          ln2_g_1, ln2_b_1, bsz, s, False)
    return xf.reshape(bsz, s, d)
```

```python
import functools
import math

import numpy as np
import jax
import jax.numpy as jnp
from jax import lax
from jax.experimental import pallas as pl
from jax.experimental.pallas import tpu as pltpu

DEPTH = 2
HEAD_DIM = 128
CONV_K = 3
MOBA_BLOCK = 256
MOBA_TOPK = 3
XATTN_HEADS = 4
REL_BUCKETS = 32
REL_MAX_DIST = 128
LN_EPS = 1e-5
NEG_INF = -1e30
DEEPNORM_ALPHA = (2 * DEPTH) ** 0.25
LOG2E = math.log2(math.e)

SB_BLOCK = 256
SB_HEADS_PER_STEP = 2
VMEM_LIMIT = 56 * 1024 * 1024

_NT = (((1,), (1,)), ((), ()))


def _cparams(n_axes):
    return pltpu.CompilerParams(dimension_semantics=("arbitrary",) * n_axes,
                                vmem_limit_bytes=VMEM_LIMIT)


def _silu(g):
    return g * jax.nn.sigmoid(g)


def _split_bf16(x):
    hi = x.astype(jnp.bfloat16)
    lo = (x - hi.astype(jnp.float32)).astype(jnp.bfloat16)
    return hi, lo


def _mm_kernel(a_ref, w_ref, o_ref, *, scaled_blocks, scale):
    acc = jnp.dot(a_ref[...], w_ref[...], preferred_element_type=jnp.float32)
    if scaled_blocks:
        acc = acc * jnp.where(pl.program_id(1) < scaled_blocks, scale, 1.0)
    o_ref[...] = acc.astype(o_ref.dtype)


def _matmul(a, w, n_start, n_size, out_dtype, bm=1024, bn=1024, scaled_cols=0, scale=1.0):
    m, k = a.shape
    bm = min(bm, m)
    bn = math.gcd(math.gcd(math.gcd(bn, n_size), n_start), scaled_cols)
    assert m % bm == 0 and bn % 128 == 0
    joff = n_start // bn
    return pl.pallas_call(
        functools.partial(_mm_kernel, scaled_blocks=scaled_cols // bn, scale=scale),
        out_shape=jax.ShapeDtypeStruct((m, n_size), out_dtype),
        grid=(m // bm, n_size // bn),
        in_specs=[pl.BlockSpec((bm, k), lambda i, j: (i, 0)),
                  pl.BlockSpec((k, bn), lambda i, j: (0, j + joff))],
        out_specs=pl.BlockSpec((bm, bn), lambda i, j: (i, j)),
        compiler_params=_cparams(2),
        name="matmul",
    )(a, w)


def _mm2_kernel(a1_ref, a2_ref, w_ref, o_ref):
    k1 = a1_ref.shape[1]
    acc = jnp.dot(a1_ref[...], w_ref[0:k1, :], preferred_element_type=jnp.float32)
    acc = acc + jnp.dot(a2_ref[...], w_ref[k1:, :], preferred_element_type=jnp.float32)
    o_ref[...] = acc.astype(o_ref.dtype)


def _matmul_cat(a1, a2, w, out_dtype, bm=1024, bn=1024):
    m, k1 = a1.shape
    k2 = a2.shape[1]
    n = w.shape[1]
    bm = min(bm, m)
    bn = min(bn, n)
    assert m % bm == 0 and n % bn == 0 and w.shape[0] == k1 + k2
    return pl.pallas_call(
        _mm2_kernel,
        out_shape=jax.ShapeDtypeStruct((m, n), out_dtype),
        grid=(m // bm, n // bn),
        in_specs=[pl.BlockSpec((bm, k1), lambda i, j: (i, 0)),
                  pl.BlockSpec((bm, k2), lambda i, j: (i, 0)),
                  pl.BlockSpec((k1 + k2, bn), lambda i, j: (0, j))],
        out_specs=pl.BlockSpec((bm, bn), lambda i, j: (i, j)),
        compiler_params=_cparams(2),
        name="matmul_cat",
    )(a1, a2, w)


def _layer_norm_rows(r, g, b):
    mu = jnp.mean(r, axis=-1, keepdims=True)
    d = r - mu
    var = jnp.mean(d * d, axis=-1, keepdims=True)
    return d * lax.rsqrt(var + LN_EPS) * g + b


def _ln_kernel(x_ref, y_ref, g_ref, b_ref, of_ref, ob_ref):
    o = _layer_norm_rows(DEEPNORM_ALPHA * x_ref[...] + y_ref[...], g_ref[...], b_ref[...])
    of_ref[...] = o
    ob_ref[...] = o.astype(jnp.bfloat16)


def _mm_ln_kernel(a_ref, w_ref, x_ref, g_ref, b_ref, of_ref, *ob_ref):
    y = jnp.dot(a_ref[...], w_ref[...], preferred_element_type=jnp.float32)
    o = _layer_norm_rows(DEEPNORM_ALPHA * x_ref[...] + y, g_ref[...], b_ref[...])
    of_ref[...] = o
    if ob_ref:
        ob_ref[0][...] = o.astype(jnp.bfloat16)


def _matmul_residual_ln(a, w, x, g, b, emit_bf16, bm=256):
    m, k = a.shape
    d = w.shape[1]
    bm = min(bm, m)
    row = pl.BlockSpec((bm, d), lambda i: (i, 0))
    vec = pl.BlockSpec((1, d), lambda i: (0, 0))
    out_shape = [jax.ShapeDtypeStruct((m, d), jnp.float32)]
    if emit_bf16:
        out_shape.append(jax.ShapeDtypeStruct((m, d), jnp.bfloat16))
    out = pl.pallas_call(
        _mm_ln_kernel,
        out_shape=tuple(out_shape),
        grid=(m // bm,),
        in_specs=[pl.BlockSpec((bm, k), lambda i: (i, 0)),
                  pl.BlockSpec((k, d), lambda i: (0, 0)), row, vec, vec],
        out_specs=tuple([row] * len(out_shape)),
        compiler_params=_cparams(1),
        name="matmul_residual_ln",
    )(a, w, x, g.reshape(1, d), b.reshape(1, d))
    return out if emit_bf16 else (out[0], None)


def _residual_ln(x, y, g, b, bm=256):
    m, d = x.shape
    bm = min(bm, m)
    row = pl.BlockSpec((bm, d), lambda i: (i, 0))
    vec = pl.BlockSpec((1, d), lambda i: (0, 0))
    return pl.pallas_call(
        _ln_kernel,
        out_shape=(jax.ShapeDtypeStruct((m, d), jnp.float32),
                   jax.ShapeDtypeStruct((m, d), jnp.bfloat16)),
        grid=(m // bm,),
        in_specs=[row, row, vec, vec],
        out_specs=(row, row),
        compiler_params=_cparams(1),
        name="residual_ln",
    )(x, y, g.reshape(1, d), b.reshape(1, d))


def _conv_proj_kernel(a_ref, wh_ref, wb_ref, wc_ref, wg_ref, cw_ref, o_ref, tail_ref, *,
                      tiles_per_seq):
    i = pl.program_id(0)
    j = pl.program_id(1)
    a = a_ref[...]
    u = (jnp.dot(a, wc_ref[...], preferred_element_type=jnp.float32)
         * jnp.dot(a, wh_ref[...], preferred_element_type=jnp.float32))
    bm = u.shape[0]

    @pl.when(i % tiles_per_seq == 0)
    def _():
        tail_ref[j] = jnp.zeros(tail_ref.shape[1:], jnp.float32)

    prev = tail_ref[j]
    tail_ref[j] = u[bm - 8:bm, :]
    row = lax.broadcasted_iota(jnp.int32, u.shape, 0)
    u1 = jnp.where(row == 0, prev[7:8, :], pltpu.roll(u, 1, 0))
    u2 = jnp.where(row == 0, prev[6:7, :],
                   jnp.where(row == 1, prev[7:8, :], pltpu.roll(u, 2, 0)))
    w = cw_ref[...]
    conv = w[0:1, :] * u2 + w[1:2, :] * u1 + w[2:3, :] * u
    b = jnp.dot(a, wb_ref[...], preferred_element_type=jnp.float32)
    g = jnp.dot(a, wg_ref[...], preferred_element_type=jnp.float32)
    o_ref[...] = (b * conv * _silu(g)).astype(o_ref.dtype)


def _conv_proj(a, w_in, conv_w, width, seq_len, bm=1024, bc=256):
    m, k = a.shape
    bm = min(bm, seq_len)
    bc = min(bc, width)
    nc = width // bc
    assert seq_len % bm == 0 and m % seq_len == 0 and width % bc == 0 and bm % 8 == 0

    def wcol(group):
        return pl.BlockSpec((k, bc), lambda i, j: (0, j + group * nc))

    return pl.pallas_call(
        functools.partial(_conv_proj_kernel, tiles_per_seq=seq_len // bm),
        out_shape=jax.ShapeDtypeStruct((m, width), jnp.bfloat16),
        grid=(m // bm, nc),
        in_specs=[pl.BlockSpec((bm, k), lambda i, j: (i, 0)),
                  wcol(0), wcol(1), wcol(2), wcol(3),
                  pl.BlockSpec((CONV_K, bc), lambda i, j: (0, j))],
        out_specs=pl.BlockSpec((bm, bc), lambda i, j: (i, j)),
        scratch_shapes=[pltpu.VMEM((nc, 8, bc), jnp.float32)],
        compiler_params=_cparams(2),
        name="conv_proj",
    )(a, w_in, w_in, w_in, w_in, conv_w)


def _t5_bucket_thresholds():
    d = np.arange(0, 4 * REL_MAX_DIST)
    max_exact = REL_BUCKETS // 2
    nf = np.maximum(d, 1).astype(np.float32)
    large = max_exact + (np.log(nf / np.float32(max_exact)) / np.float32(math.log(REL_MAX_DIST / max_exact))
                         * np.float32(REL_BUCKETS - max_exact)).astype(np.int32)
    large = np.minimum(large, REL_BUCKETS - 1)
    bucket = np.where(d < max_exact, d, large)
    assert np.all(np.diff(bucket) >= 0) and bucket[-1] == REL_BUCKETS - 1
    return [int(np.argmax(bucket >= b)) for b in range(REL_BUCKETS)]


_BUCKET_THR = _t5_bucket_thresholds()
_MOBA_BLOCK_SHIFT = MOBA_BLOCK.bit_length() - 1
assert 1 << _MOBA_BLOCK_SHIFT == MOBA_BLOCK


def _moba_kernel(rb_ref, q_ref, k_ref, v_ref, g_ref, o_ref, t0_ref, t1_ref, kaug_ref, *, nb):
    blk = MOBA_BLOCK
    h = pl.program_id(0)
    s_len = nb * blk
    qo = lax.broadcasted_iota(jnp.int32, (blk, blk), 0)
    ko = lax.broadcasted_iota(jnp.int32, (blk, blk), 1)

    @pl.when(pl.program_id(1) == 0)
    def _():
        def table(dist):
            t = jnp.full(dist.shape, rb_ref[h, REL_BUCKETS - 1], jnp.float32)
            for bkt in range(REL_BUCKETS - 2, -1, -1):
                t = jnp.where(dist < _BUCKET_THR[bkt + 1], rb_ref[h, bkt], t)
            return t * LOG2E
        t0_ref[...] = table(qo - ko)
        t1_ref[...] = table(qo - ko + blk)

    rb_far = rb_ref[h, REL_BUCKETS - 1] * LOG2E
    assert _BUCKET_THR[REL_BUCKETS - 1] <= blk + 1

    kaug_ref[:, 0:HEAD_DIM] = k_ref[0]
    kaug_ref[:, HEAD_DIM:2 * HEAD_DIM] = jnp.where(
        jnp.right_shift(lax.broadcasted_iota(jnp.int32, (s_len, HEAD_DIM), 0), _MOBA_BLOCK_SHIFT)
        == lax.broadcasted_iota(jnp.int32, (s_len, HEAD_DIM), 1), 1.0, 0.0).astype(jnp.bfloat16)

    kbar = jnp.concatenate(
        [jnp.mean(k_ref[0, j * blk:(j + 1) * blk, :].astype(jnp.float32), axis=0, keepdims=True)
         for j in range(nb)] + [jnp.zeros((16 - nb, HEAD_DIM), jnp.float32)], axis=0)
    kbar_hi, kbar_lo = _split_bf16(kbar)

    eye = jnp.where(qo == ko, 1.0, 0.0).astype(jnp.bfloat16)
    jidx = lax.broadcasted_iota(jnp.int32, (16, blk), 0)

    def logits(i):
        q_i = q_ref[0, i * blk:(i + 1) * blk, :]
        nk = (i + 1) * blk
        if i > MOBA_TOPK:
            gate = (lax.dot_general(kbar_hi, q_i, _NT, preferred_element_type=jnp.float32)
                    + lax.dot_general(kbar_lo, q_i, _NT, preferred_element_type=jnp.float32))
            cnt = jnp.zeros((16, blk), jnp.float32)
            for jp in range(i):
                row = gate[jp:jp + 1, :]
                beats = jnp.where(row > gate, 1.0,
                                  jnp.where(row == gate, jnp.where(jidx > jp, 1.0, 0.0), 0.0))
                cnt = cnt + beats
            drop_t = jnp.where(cnt < MOBA_TOPK, 0.0, jnp.where(jidx < i, NEG_INF, 0.0))
            drop_t = jnp.concatenate([drop_t, jnp.zeros((128 - 16, blk), jnp.float32)], axis=0)
            drop = lax.dot_general(eye, drop_t.astype(jnp.bfloat16), _NT,
                                   preferred_element_type=jnp.float32)
            q_aug = jnp.concatenate([q_i, drop.astype(jnp.bfloat16)], axis=-1)
            return lax.dot_general(q_aug, kaug_ref[0:nk, :], _NT,
                                   preferred_element_type=jnp.float32)
        return lax.dot_general(q_i, k_ref[0, 0:nk, :], _NT, preferred_element_type=jnp.float32)

    def attend(i, s):
        nk = (i + 1) * blk
        n_far = max(i - 1, 0) * blk
        lg_own = jnp.where(ko <= qo, s[:, i * blk:nk] + t0_ref[...], NEG_INF)
        m = jnp.max(lg_own, axis=-1, keepdims=True)
        if i > 0:
            lg_prev = s[:, (i - 1) * blk:i * blk] + t1_ref[...]
            m = jnp.maximum(m, jnp.max(lg_prev, axis=-1, keepdims=True))
        if n_far:
            m = jnp.maximum(m, jnp.max(s[:, 0:n_far], axis=-1, keepdims=True) + rb_far)
        pieces = [jnp.exp2(lg_own - m)]
        if i > 0:
            pieces.insert(0, jnp.exp2(lg_prev - m))
        if n_far:
            pieces.insert(0, jnp.exp2(s[:, 0:n_far] - (m - rb_far)))
        l = sum(jnp.sum(piece, axis=-1, keepdims=True) for piece in pieces)
        p = pieces[0] if len(pieces) == 1 else jnp.concatenate(pieces, axis=-1)
        o = jnp.dot(p.astype(jnp.bfloat16), v_ref[0, 0:nk, :],
                    preferred_element_type=jnp.float32) / l
        o_ref[0, i * blk:(i + 1) * blk, :] = (o * _silu(g_ref[0, i * blk:(i + 1) * blk, :])
                                              ).astype(o_ref.dtype)

    s_next = logits(0)
    for i in range(nb):
        s_cur, s_next = s_next, (logits(i + 1) if i + 1 < nb else None)
        attend(i, s_cur)


def _moba(qkv, g, rel_bias_t, heads):
    bsz, s, _ = qkv.shape
    assert s % MOBA_BLOCK == 0
    nb = s // MOBA_BLOCK
    assert MOBA_TOPK < nb <= 16

    def col(group):
        return pl.BlockSpec((1, s, HEAD_DIM), lambda h, b: (b, 0, h + group * heads))

    return pl.pallas_call(
        functools.partial(_moba_kernel, nb=nb),
        out_shape=jax.ShapeDtypeStruct((bsz, s, heads * HEAD_DIM), jnp.bfloat16),
        grid=(heads, bsz),
        in_specs=[pl.BlockSpec(memory_space=pltpu.SMEM), col(0), col(1), col(2),
                  pl.BlockSpec((1, s, HEAD_DIM), lambda h, b: (b, 0, h))],
        out_specs=pl.BlockSpec((1, s, HEAD_DIM), lambda h, b: (b, 0, h)),
        scratch_shapes=[pltpu.VMEM((MOBA_BLOCK, MOBA_BLOCK), jnp.float32),
                        pltpu.VMEM((MOBA_BLOCK, MOBA_BLOCK), jnp.float32),
                        pltpu.VMEM((s, 2 * HEAD_DIM), jnp.bfloat16)],
        compiler_params=_cparams(2),
        name="moba",
    )(rel_bias_t, qkv, qkv, qkv, g)


def _sb_kernel(q_ref, k_ref, v_ref, g_ref, o_ref, za_ref, zb_ref, *, nq):
    t = SB_BLOCK
    tq = 2 * t
    tri = jnp.where(lax.broadcasted_iota(jnp.int32, (t, t), 0)
                    > lax.broadcasted_iota(jnp.int32, (t, t), 1), 1.0, 0.0).astype(jnp.bfloat16)
    tri2 = jnp.concatenate([tri, tri], axis=0)

    def logits(qs, j0):
        return tuple(lax.dot_general(q, k_ref[0, pl.ds(j0, t), hh * HEAD_DIM:(hh + 1) * HEAD_DIM],
                                     _NT, preferred_element_type=jnp.float32)
                     for hh, q in enumerate(qs))

    def tiles(zs, j0, state, diag):
        heads = range(len(zs))
        cols = [slice(hh * HEAD_DIM, (hh + 1) * HEAD_DIM) for hh in heads]
        if diag:
            strict = (lax.broadcasted_iota(jnp.int32, zs[0].shape, 1)
                      < lax.broadcasted_iota(jnp.int32, zs[0].shape, 0))
            zs = [jnp.where(strict, z, NEG_INF) for z in zs]
        sps = [jnp.where(z > 30.0, z, jnp.log(1.0 + jnp.exp2(z)) * LOG2E) for z in zs]
        rems = [jnp.dot(sps[hh].astype(jnp.bfloat16), tri,
                        preferred_element_type=jnp.float32) + state[2 * hh] for hh in heads]
        ws = [jnp.exp2((zs[hh] - sps[hh]) - rems[hh]).astype(jnp.bfloat16) for hh in heads]
        out = []
        for hh in heads:
            acc = state[2 * hh + 1] + jnp.dot(ws[hh], v_ref[0, pl.ds(j0, t), cols[hh]],
                                              preferred_element_type=jnp.float32)
            out.extend((rems[hh][:, 0:1] + sps[hh][:, 0:1], acc))
        return tuple(out)

    def q_block(i, _):
        q0 = pl.multiple_of(i * tq, tq)
        q1 = pl.multiple_of(i * tq + t, t)
        heads = range(SB_HEADS_PER_STEP)
        cols = [slice(hh * HEAD_DIM, (hh + 1) * HEAD_DIM) for hh in heads]
        zero_c = jnp.zeros((t, 1), jnp.float32)
        zero_a = jnp.zeros((t, HEAD_DIM), jnp.float32)
        q_i = [q_ref[0, pl.ds(q0, tq), c] for c in cols]

        def past_j0(n):
            return pl.multiple_of(jnp.maximum(2 * i - 1 - n, 0) * t, t)

        def stage_logits(z_ref, n):
            for hh, z in enumerate(logits(q_i, past_j0(n))):
                z_ref[hh] = z

        zs_low = logits([q_ref[0, pl.ds(q1, t), c] for c in cols], q1)
        zs_diag = logits(q_i, q0)
        stage_logits(za_ref, 0)
        low = tiles(zs_low, q1, (zero_c, zero_a) * len(cols), True)
        state = []
        for hh in heads:
            state.append(jnp.concatenate([zero_c, low[2 * hh]], axis=0))
            state.append(jnp.concatenate([zero_a, low[2 * hh + 1]], axis=0))
        state = tiles(zs_diag, q0, tuple(state), True)

        def past_pair(m, state):
            stage_logits(zb_ref, 2 * m + 1)
            state = tiles([za_ref[hh] for hh in heads], past_j0(2 * m), state, False)
            stage_logits(za_ref, 2 * m + 2)
            return tiles([zb_ref[hh] for hh in heads], past_j0(2 * m + 1), state, False)

        state = lax.fori_loop(0, i, past_pair, state)
        acc = jnp.concatenate([state[2 * hh + 1] for hh in heads], axis=-1)
        o_ref[0, pl.ds(q0, tq), :] = (acc * _silu(g_ref[0, pl.ds(q0, tq), :])).astype(o_ref.dtype)
        return 0

    lax.fori_loop(0, nq, q_block, 0)


def _stick_breaking(qkv, g, heads):
    bsz, s, _ = qkv.shape
    assert s % (2 * SB_BLOCK) == 0 and heads % SB_HEADS_PER_STEP == 0
    steps = heads // SB_HEADS_PER_STEP
    width = SB_HEADS_PER_STEP * HEAD_DIM

    def col(group):
        return pl.BlockSpec((1, s, width), lambda h, b: (b, 0, h + group * steps))

    return pl.pallas_call(
        functools.partial(_sb_kernel, nq=s // (2 * SB_BLOCK)),
        out_shape=jax.ShapeDtypeStruct((bsz, s, heads * HEAD_DIM), jnp.bfloat16),
        grid=(steps, bsz),
        in_specs=[col(0), col(1), col(2), col(0)],
        out_specs=col(0),
        scratch_shapes=[pltpu.VMEM((SB_HEADS_PER_STEP, 2 * SB_BLOCK, SB_BLOCK), jnp.float32)] * 2,
        compiler_params=_cparams(2),
        name="stick_breaking",
    )(qkv, qkv, qkv, g)


def _xattn_kernel(q_ref, g_ref, k_ref, v_ref, o_ref, *, dh):
    scale = dh ** -0.5
    for h in range(XATTN_HEADS):
        cols = slice(h * dh, (h + 1) * dh)
        s = lax.dot_general(q_ref[0, :, cols], k_ref[0, :, cols], _NT,
                            preferred_element_type=jnp.float32) * scale
        m = jnp.max(s, axis=-1, keepdims=True)
        p = jnp.exp(s - m)
        l = jnp.sum(p, axis=-1, keepdims=True)
        o = jnp.dot(p.astype(jnp.bfloat16), v_ref[0, :, cols],
                    preferred_element_type=jnp.float32) / l
        o_ref[0, :, cols] = (o * _silu(g_ref[0, :, cols])).astype(o_ref.dtype)


def _xattn(q, g, kv, bm=512):
    bsz, s, w = q.shape
    n_mem = kv.shape[1]
    bm = min(bm, s)
    row = pl.BlockSpec((1, bm, w), lambda b, i: (b, i, 0))
    return pl.pallas_call(
        functools.partial(_xattn_kernel, dh=w // XATTN_HEADS),
        out_shape=jax.ShapeDtypeStruct((bsz, s, w), jnp.bfloat16),
        grid=(bsz, s // bm),
        in_specs=[row, row,
                  pl.BlockSpec((1, n_mem, w), lambda b, i: (b, 0, 0)),
                  pl.BlockSpec((1, n_mem, w), lambda b, i: (b, 0, 1))],
        out_specs=row,
        compiler_params=_cparams(2),
        name="xattn",
    )(q, g, kv, kv)


def _memory_block(x_f32, x_bf16, mem_bf16, w_xq, w_xkv, w_xo, g2, b2, bsz, s, emit_bf16):
    xw = w_xo.shape[0]
    q = _matmul(x_bf16, w_xq, 0, xw, jnp.bfloat16)
    g = _matmul(x_bf16, w_xq, xw, xw, jnp.float32)
    kv = _matmul(mem_bf16, w_xkv, 0, 2 * xw, jnp.bfloat16)
    n_mem = mem_bf16.shape[0] // bsz
    o = _xattn(q.reshape(bsz, s, xw), g.reshape(bsz, s, xw), kv.reshape(bsz, n_mem, 2 * xw))
    return _matmul_residual_ln(o.reshape(bsz * s, xw), w_xo, x_f32, g2, b2, emit_bf16)


def kernel(x, mem, rel_bias, w_in_0, conv_w_0, w_out_0, ln1_g_0, ln1_b_0, w_xq_0, w_xkv_0, w_xo_0, ln2_g_0, ln2_b_0, w_in_1, w_out_1, ln1_g_1, ln1_b_1, w_xq_1, w_xkv_1, w_xo_1, ln2_g_1, ln2_b_1):
    bsz, s, d = x.shape
    bf = jnp.bfloat16
    xf = x.reshape(bsz * s, d)
    xb = xf.astype(bf)
    memb = mem.reshape(-1, d).astype(bf)

    cw = conv_w_0.shape[1]
    mw = w_in_0.shape[1] // 4 - cw
    heads = mw // HEAD_DIM
    w_in = w_in_0.astype(bf)
    y_a = _conv_proj(xb, w_in, conv_w_0, cw, s)
    qkv = _matmul(xb, w_in, 4 * cw, 3 * mw, bf, scaled_cols=mw, scale=HEAD_DIM ** -0.5 * LOG2E)
    g_b = _matmul(xb, w_in, 4 * cw + 3 * mw, mw, jnp.float32)
    y_b = _moba(qkv.reshape(bsz, s, 3 * mw), g_b.reshape(bsz, s, mw), rel_bias.T, heads)
    y = _matmul_cat(y_a, y_b.reshape(bsz * s, mw), w_out_0.astype(bf),
                    jnp.float32)
    xf, xb = _residual_ln(xf, y, ln1_g_0, ln1_b_0)
    xf, xb = _memory_block(xf, xb, memb, w_xq_0.astype(bf), w_xkv_0.astype(bf), w_xo_0.astype(bf),
                           ln2_g_0, ln2_b_0, bsz, s, True)

    sw = w_out_1.shape[0]
    w_in = w_in_1.astype(bf)
    qkv = _matmul(xb, w_in, 0, 3 * sw, bf, scaled_cols=sw, scale=HEAD_DIM ** -0.5 * LOG2E)
    g_c = _matmul(xb, w_in, 3 * sw, sw, jnp.float32)
    y_c = _stick_breaking(qkv.reshape(bsz, s, 3 * sw), g_c.reshape(bsz, s, sw), sw // HEAD_DIM)
    y = _matmul(y_c.reshape(bsz * s, sw), w_out_1.astype(bf), 0, d, jnp.float32)
    xf, xb = _residual_ln(xf, y, ln1_g_1, ln1_b_1)
    xf, _ = _memory_block(xf, xb, memb, w_xq_1.astype(bf), w_xkv_1.astype(bf), w_xo_1.astype(bf),
                          ln2_g_1, ln2_b_1, bsz, s, False)
    return xf.reshape(bsz, s, d)
```

```python
import functools
import math

import numpy as np
import jax
import jax.numpy as jnp
from jax import lax
from jax.experimental import pallas as pl
from jax.experimental.pallas import tpu as pltpu

DEPTH = 2
HEAD_DIM = 128
CONV_K = 3
MOBA_BLOCK = 256
MOBA_TOPK = 3
XATTN_HEADS = 4
REL_BUCKETS = 32
REL_MAX_DIST = 128
LN_EPS = 1e-5
NEG_INF = -1e30
DEEPNORM_ALPHA = (2 * DEPTH) ** 0.25
LOG2E = math.log2(math.e)

SB_BLOCK = 256
SB_HEADS_PER_STEP = 4
SB_MAX_LOG2_ODDS = 60.0
VMEM_LIMIT = 56 * 1024 * 1024

_NT = (((1,), (1,)), ((), ()))


def _cparams(n_axes):
    return pltpu.CompilerParams(dimension_semantics=("arbitrary",) * n_axes,
                                vmem_limit_bytes=VMEM_LIMIT)


def _silu(g):
    return g * jax.nn.sigmoid(g)


def _split_bf16(x):
    hi = x.astype(jnp.bfloat16)
    lo = (x - hi.astype(jnp.float32)).astype(jnp.bfloat16)
    return hi, lo


def _mm_kernel(a_ref, w_ref, o_ref, *, scaled_blocks, scale):
    acc = jnp.dot(a_ref[...], w_ref[...], preferred_element_type=jnp.float32)
    if scaled_blocks:
        acc = acc * jnp.where(pl.program_id(1) < scaled_blocks, scale, 1.0)
    o_ref[...] = acc.astype(o_ref.dtype)


def _matmul(a, w, n_start, n_size, out_dtype, bm=1024, bn=1024, scaled_cols=0, scale=1.0):
    m, k = a.shape
    bm = min(bm, m)
    bn = math.gcd(math.gcd(math.gcd(bn, n_size), n_start), scaled_cols)
    assert m % bm == 0 and bn % 128 == 0
    joff = n_start // bn
    return pl.pallas_call(
        functools.partial(_mm_kernel, scaled_blocks=scaled_cols // bn, scale=scale),
        out_shape=jax.ShapeDtypeStruct((m, n_size), out_dtype),
        grid=(m // bm, n_size // bn),
        in_specs=[pl.BlockSpec((bm, k), lambda i, j: (i, 0)),
                  pl.BlockSpec((k, bn), lambda i, j: (0, j + joff))],
        out_specs=pl.BlockSpec((bm, bn), lambda i, j: (i, j)),
        compiler_params=_cparams(2),
        name="matmul",
    )(a, w)


def _mm2_kernel(a1_ref, a2_ref, w_ref, o_ref):
    k1 = a1_ref.shape[1]
    acc = jnp.dot(a1_ref[...], w_ref[0:k1, :], preferred_element_type=jnp.float32)
    acc = acc + jnp.dot(a2_ref[...], w_ref[k1:, :], preferred_element_type=jnp.float32)
    o_ref[...] = acc.astype(o_ref.dtype)


def _matmul_cat(a1, a2, w, out_dtype, bm=1024, bn=1024):
    m, k1 = a1.shape
    k2 = a2.shape[1]
    n = w.shape[1]
    bm = min(bm, m)
    bn = min(bn, n)
    assert m % bm == 0 and n % bn == 0 and w.shape[0] == k1 + k2
    return pl.pallas_call(
        _mm2_kernel,
        out_shape=jax.ShapeDtypeStruct((m, n), out_dtype),
        grid=(m // bm, n // bn),
        in_specs=[pl.BlockSpec((bm, k1), lambda i, j: (i, 0)),
                  pl.BlockSpec((bm, k2), lambda i, j: (i, 0)),
                  pl.BlockSpec((k1 + k2, bn), lambda i, j: (0, j))],
        out_specs=pl.BlockSpec((bm, bn), lambda i, j: (i, j)),
        compiler_params=_cparams(2),
        name="matmul_cat",
    )(a1, a2, w)


def _layer_norm_rows(r, g, b):
    mu = jnp.mean(r, axis=-1, keepdims=True)
    d = r - mu
    var = jnp.mean(d * d, axis=-1, keepdims=True)
    return d * lax.rsqrt(var + LN_EPS) * g + b


def _ln_kernel(x_ref, y_ref, g_ref, b_ref, of_ref, ob_ref):
    o = _layer_norm_rows(DEEPNORM_ALPHA * x_ref[...] + y_ref[...], g_ref[...], b_ref[...])
    of_ref[...] = o
    ob_ref[...] = o.astype(jnp.bfloat16)


def _mm_ln_kernel(a_ref, w_ref, x_ref, g_ref, b_ref, of_ref, *ob_ref):
    y = jnp.dot(a_ref[...], w_ref[...], preferred_element_type=jnp.float32)
    o = _layer_norm_rows(DEEPNORM_ALPHA * x_ref[...] + y, g_ref[...], b_ref[...])
    of_ref[...] = o
    if ob_ref:
        ob_ref[0][...] = o.astype(jnp.bfloat16)


def _mm_ln_ktiled_kernel(a1_ref, a2_ref, w_ref, x_ref, g_ref, b_ref, of_ref, ob_ref, acc_ref, *,
                         nk1, nk):
    k = pl.program_id(1)
    half = of_ref.shape[1]
    rows_per_chunk = 64

    @pl.when(k == 0)
    def _():
        acc_ref[...] = jnp.dot(a1_ref[...], w_ref[...], preferred_element_type=jnp.float32)

    @pl.when((k > 0) & (k < nk1))
    def _():
        acc_ref[...] += jnp.dot(a1_ref[...], w_ref[...], preferred_element_type=jnp.float32)

    @pl.when((k >= nk1) & (k < nk))
    def _():
        acc_ref[...] += jnp.dot(a2_ref[...], w_ref[...], preferred_element_type=jnp.float32)

    @pl.when(k == nk - 1)
    def _():
        def chunk(c, _):
            rows = pl.ds(pl.multiple_of(c * rows_per_chunk, rows_per_chunk), rows_per_chunk)
            o = _layer_norm_rows(DEEPNORM_ALPHA * x_ref[rows, :] + acc_ref[rows, :],
                                 g_ref[...], b_ref[...])
            acc_ref[rows, :] = o
            of_ref[rows, :] = o[:, :half]
            ob_ref[rows, :] = o[:, :half].astype(jnp.bfloat16)
            return 0
        lax.fori_loop(0, acc_ref.shape[0] // rows_per_chunk, chunk, 0)

    @pl.when(k == nk)
    def _():
        of_ref[...] = acc_ref[:, half:]
        ob_ref[...] = acc_ref[:, half:].astype(jnp.bfloat16)


def _matmul_residual_ln_ktiled(a1, k1, a2, a2_col0, w, x, g, b, bm=512, bk=512):
    m = a1.shape[0]
    kdim, d = w.shape
    k2 = kdim - k1
    bm = min(bm, m)
    bk = math.gcd(bk, k1, k2, a2_col0)
    nk1, nk2, nk = k1 // bk, k2 // bk, kdim // bk
    off2 = a2_col0 // bk
    half = d // 2
    assert m % bm == 0 and bm % 64 == 0 and half % 128 == 0 and a2.shape[1] == a2_col0 + k2
    out_spec = pl.BlockSpec((bm, half), lambda i, k: (i, jnp.maximum(k - (nk - 1), 0)))
    vec = pl.BlockSpec((1, d), lambda i, k: (0, 0))
    return pl.pallas_call(
        functools.partial(_mm_ln_ktiled_kernel, nk1=nk1, nk=nk),
        out_shape=(jax.ShapeDtypeStruct((m, d), jnp.float32),
                   jax.ShapeDtypeStruct((m, d), jnp.bfloat16)),
        grid=(m // bm, nk + 1),
        in_specs=[pl.BlockSpec((bm, bk), lambda i, k: (i, jnp.minimum(k, nk1 - 1))),
                  pl.BlockSpec((bm, bk), lambda i, k: (i, off2 + jnp.clip(k - nk1, 0, nk2 - 1))),
                  pl.BlockSpec((bk, d), lambda i, k: (jnp.minimum(k, nk - 1), 0)),
                  pl.BlockSpec((bm, d), lambda i, k: (i, 0)), vec, vec],
        out_specs=(out_spec, out_spec),
        scratch_shapes=[pltpu.VMEM((bm, d), jnp.float32)],
        compiler_params=_cparams(2),
        name="matmul_residual_ln_ktiled",
    )(a1, a2, w, x, g.reshape(1, d), b.reshape(1, d))


def _matmul_residual_ln(a, w, x, g, b, emit_bf16, bm=256):
    m, k = a.shape
    d = w.shape[1]
    bm = min(bm, m)
    row = pl.BlockSpec((bm, d), lambda i: (i, 0))
    vec = pl.BlockSpec((1, d), lambda i: (0, 0))
    out_shape = [jax.ShapeDtypeStruct((m, d), jnp.float32)]
    if emit_bf16:
        out_shape.append(jax.ShapeDtypeStruct((m, d), jnp.bfloat16))
    out = pl.pallas_call(
        _mm_ln_kernel,
        out_shape=tuple(out_shape),
        grid=(m // bm,),
        in_specs=[pl.BlockSpec((bm, k), lambda i: (i, 0)),
                  pl.BlockSpec((k, d), lambda i: (0, 0)), row, vec, vec],
        out_specs=tuple([row] * len(out_shape)),
        compiler_params=_cparams(1),
        name="matmul_residual_ln",
    )(a, w, x, g.reshape(1, d), b.reshape(1, d))
    return out if emit_bf16 else (out[0], None)


def _residual_ln(x, y, g, b, bm=256):
    m, d = x.shape
    bm = min(bm, m)
    row = pl.BlockSpec((bm, d), lambda i: (i, 0))
    vec = pl.BlockSpec((1, d), lambda i: (0, 0))
    return pl.pallas_call(
        _ln_kernel,
        out_shape=(jax.ShapeDtypeStruct((m, d), jnp.float32),
                   jax.ShapeDtypeStruct((m, d), jnp.bfloat16)),
        grid=(m // bm,),
        in_specs=[row, row, vec, vec],
        out_specs=(row, row),
        compiler_params=_cparams(1),
        name="residual_ln",
    )(x, y, g.reshape(1, d), b.reshape(1, d))


def _conv_proj_kernel(a_ref, wh_ref, wb_ref, wc_ref, wg_ref, cw_ref, o_ref, tail_ref, *,
                      tiles_per_seq):
    i = pl.program_id(0)
    j = pl.program_id(1)
    a = a_ref[...]
    u = (jnp.dot(a, wc_ref[...], preferred_element_type=jnp.float32)
         * jnp.dot(a, wh_ref[...], preferred_element_type=jnp.float32))
    bm = u.shape[0]

    @pl.when(i % tiles_per_seq == 0)
    def _():
        tail_ref[j] = jnp.zeros(tail_ref.shape[1:], jnp.float32)

    prev = tail_ref[j]
    tail_ref[j] = u[bm - 8:bm, :]
    row = lax.broadcasted_iota(jnp.int32, u.shape, 0)
    u1 = jnp.where(row == 0, prev[7:8, :], pltpu.roll(u, 1, 0))
    u2 = jnp.where(row == 0, prev[6:7, :],
                   jnp.where(row == 1, prev[7:8, :], pltpu.roll(u, 2, 0)))
    w = cw_ref[...]
    conv = w[0:1, :] * u2 + w[1:2, :] * u1 + w[2:3, :] * u
    b = jnp.dot(a, wb_ref[...], preferred_element_type=jnp.float32)
    g = jnp.dot(a, wg_ref[...], preferred_element_type=jnp.float32)
    o_ref[...] = (b * conv * _silu(g)).astype(o_ref.dtype)


def _conv_proj(a, w_in, conv_w, width, seq_len, bm=1024, bc=256):
    m, k = a.shape
    bm = min(bm, seq_len)
    bc = min(bc, width)
    nc = width // bc
    assert seq_len % bm == 0 and m % seq_len == 0 and width % bc == 0 and bm % 8 == 0

    def wcol(group):
        return pl.BlockSpec((k, bc), lambda i, j: (0, j + group * nc))

    return pl.pallas_call(
        functools.partial(_conv_proj_kernel, tiles_per_seq=seq_len // bm),
        out_shape=jax.ShapeDtypeStruct((m, width), jnp.bfloat16),
        grid=(m // bm, nc),
        in_specs=[pl.BlockSpec((bm, k), lambda i, j: (i, 0)),
                  wcol(0), wcol(1), wcol(2), wcol(3),
                  pl.BlockSpec((CONV_K, bc), lambda i, j: (0, j))],
        out_specs=pl.BlockSpec((bm, bc), lambda i, j: (i, j)),
        scratch_shapes=[pltpu.VMEM((nc, 8, bc), jnp.float32)],
        compiler_params=_cparams(2),
        name="conv_proj",
    )(a, w_in, w_in, w_in, w_in, conv_w)


def _t5_bucket_thresholds():
    d = np.arange(0, 4 * REL_MAX_DIST)
    max_exact = REL_BUCKETS // 2
    nf = np.maximum(d, 1).astype(np.float32)
    large = max_exact + (np.log(nf / np.float32(max_exact)) / np.float32(math.log(REL_MAX_DIST / max_exact))
                         * np.float32(REL_BUCKETS - max_exact)).astype(np.int32)
    large = np.minimum(large, REL_BUCKETS - 1)
    bucket = np.where(d < max_exact, d, large)
    assert np.all(np.diff(bucket) >= 0) and bucket[-1] == REL_BUCKETS - 1
    return [int(np.argmax(bucket >= b)) for b in range(REL_BUCKETS)]


_BUCKET_THR = _t5_bucket_thresholds()
_MOBA_BLOCK_SHIFT = MOBA_BLOCK.bit_length() - 1
assert 1 << _MOBA_BLOCK_SHIFT == MOBA_BLOCK


def _moba_kernel(rb_ref, q_ref, k_ref, v_ref, g_ref, o_ref, t0_ref, t1_ref, kaug_ref, *, nb):
    blk = MOBA_BLOCK
    h = pl.program_id(0)
    s_len = nb * blk
    qo = lax.broadcasted_iota(jnp.int32, (blk, blk), 0)
    ko = lax.broadcasted_iota(jnp.int32, (blk, blk), 1)

    @pl.when(pl.program_id(1) == 0)
    def _():
        def table(dist):
            t = jnp.full(dist.shape, rb_ref[h, REL_BUCKETS - 1], jnp.float32)
            for bkt in range(REL_BUCKETS - 2, -1, -1):
                t = jnp.where(dist < _BUCKET_THR[bkt + 1], rb_ref[h, bkt], t)
            return t * LOG2E
        t0_ref[...] = table(qo - ko)
        t1_ref[...] = table(qo - ko + blk)

    rb_far = rb_ref[h, REL_BUCKETS - 1] * LOG2E
    assert _BUCKET_THR[REL_BUCKETS - 1] <= blk + 1

    kaug_ref[:, 0:HEAD_DIM] = k_ref[0]
    kaug_ref[:, HEAD_DIM:2 * HEAD_DIM] = jnp.where(
        jnp.right_shift(lax.broadcasted_iota(jnp.int32, (s_len, HEAD_DIM), 0), _MOBA_BLOCK_SHIFT)
        == lax.broadcasted_iota(jnp.int32, (s_len, HEAD_DIM), 1), 1.0, 0.0).astype(jnp.bfloat16)

    kbar = jnp.concatenate(
        [jnp.mean(k_ref[0, j * blk:(j + 1) * blk, :].astype(jnp.float32), axis=0, keepdims=True)
         for j in range(nb)] + [jnp.zeros((16 - nb, HEAD_DIM), jnp.float32)], axis=0)
    kbar_hi, kbar_lo = _split_bf16(kbar)

    eye = jnp.where(qo == ko, 1.0, 0.0).astype(jnp.bfloat16)
    jidx = lax.broadcasted_iota(jnp.int32, (16, blk), 0)

    def logits(i):
        q_i = q_ref[0, i * blk:(i + 1) * blk, :]
        nk = (i + 1) * blk
        if i > MOBA_TOPK:
            gate = (lax.dot_general(kbar_hi, q_i, _NT, preferred_element_type=jnp.float32)
                    + lax.dot_general(kbar_lo, q_i, _NT, preferred_element_type=jnp.float32))
            cnt = jnp.zeros((16, blk), jnp.float32)
            for jp in range(i):
                row = gate[jp:jp + 1, :]
                beats = jnp.where(row > gate, 1.0,
                                  jnp.where(row == gate, jnp.where(jidx > jp, 1.0, 0.0), 0.0))
                cnt = cnt + beats
            drop_t = jnp.where(cnt < MOBA_TOPK, 0.0, jnp.where(jidx < i, NEG_INF, 0.0))
            drop_t = jnp.concatenate([drop_t, jnp.zeros((128 - 16, blk), jnp.float32)], axis=0)
            drop = lax.dot_general(eye, drop_t.astype(jnp.bfloat16), _NT,
                                   preferred_element_type=jnp.float32)
            q_aug = jnp.concatenate([q_i, drop.astype(jnp.bfloat16)], axis=-1)
            return lax.dot_general(q_aug, kaug_ref[0:nk, :], _NT,
                                   preferred_element_type=jnp.float32)
        return lax.dot_general(q_i, k_ref[0, 0:nk, :], _NT, preferred_element_type=jnp.float32)

    def attend(i, s):
        nk = (i + 1) * blk
        n_far = max(i - 1, 0) * blk
        lg_own = jnp.where(ko <= qo, s[:, i * blk:nk] + t0_ref[...], NEG_INF)
        m = jnp.max(lg_own, axis=-1, keepdims=True)
        if i > 0:
            lg_prev = s[:, (i - 1) * blk:i * blk] + t1_ref[...]
            m = jnp.maximum(m, jnp.max(lg_prev, axis=-1, keepdims=True))
        if n_far:
            m = jnp.maximum(m, jnp.max(s[:, 0:n_far], axis=-1, keepdims=True) + rb_far)
        pieces = [jnp.exp2(lg_own - m)]
        if i > 0:
            pieces.insert(0, jnp.exp2(lg_prev - m))
        if n_far:
            pieces.insert(0, jnp.exp2(s[:, 0:n_far] - (m - rb_far)))
        l = sum(jnp.sum(piece, axis=-1, keepdims=True) for piece in pieces)
        p = pieces[0] if len(pieces) == 1 else jnp.concatenate(pieces, axis=-1)
        o = jnp.dot(p.astype(jnp.bfloat16), v_ref[0, 0:nk, :],
                    preferred_element_type=jnp.float32) / l
        o_ref[0, i * blk:(i + 1) * blk, :] = (o * _silu(g_ref[0, i * blk:(i + 1) * blk, :])
                                              ).astype(o_ref.dtype)

    s_next = logits(0)
    for i in range(nb):
        s_cur, s_next = s_next, (logits(i + 1) if i + 1 < nb else None)
        attend(i, s_cur)


def _moba(qkv, g, rel_bias_t, heads):
    bsz, s, _ = qkv.shape
    assert s % MOBA_BLOCK == 0
    nb = s // MOBA_BLOCK
    assert MOBA_TOPK < nb <= 16

    def col(group):
        return pl.BlockSpec((1, s, HEAD_DIM), lambda h, b: (b, 0, h + group * heads))

    return pl.pallas_call(
        functools.partial(_moba_kernel, nb=nb),
        out_shape=jax.ShapeDtypeStruct((bsz, s, heads * HEAD_DIM), jnp.bfloat16),
        grid=(heads, bsz),
        in_specs=[pl.BlockSpec(memory_space=pltpu.SMEM), col(0), col(1), col(2),
                  pl.BlockSpec((1, s, HEAD_DIM), lambda h, b: (b, 0, h))],
        out_specs=pl.BlockSpec((1, s, HEAD_DIM), lambda h, b: (b, 0, h)),
        scratch_shapes=[pltpu.VMEM((MOBA_BLOCK, MOBA_BLOCK), jnp.float32),
                        pltpu.VMEM((MOBA_BLOCK, MOBA_BLOCK), jnp.float32),
                        pltpu.VMEM((s, 2 * HEAD_DIM), jnp.bfloat16)],
        compiler_params=_cparams(2),
        name="moba",
    )(rel_bias_t, qkv, qkv, qkv, g)


def _sb_kernel(q_ref, k_ref, v_ref, g_ref, o_ref, za_ref, zb_ref, acc_ref, car_ref, *, nq):
    t = SB_BLOCK
    tq = 2 * t
    tri = jnp.where(lax.broadcasted_iota(jnp.int32, (t, t), 0)
                    > lax.broadcasted_iota(jnp.int32, (t, t), 1), 1.0, 0.0).astype(jnp.bfloat16)

    def logits(qs, j0):
        return tuple(lax.dot_general(q, k_ref[0, pl.ds(j0, t), hh * HEAD_DIM:(hh + 1) * HEAD_DIM],
                                     _NT, preferred_element_type=jnp.float32)
                     for hh, q in enumerate(qs))

    def tiles(zs, j0, rows, diag):
        heads = range(len(zs))
        cols = [slice(hh * HEAD_DIM, (hh + 1) * HEAD_DIM) for hh in heads]
        if diag:
            strict = (lax.broadcasted_iota(jnp.int32, zs[0].shape, 1)
                      < lax.broadcasted_iota(jnp.int32, zs[0].shape, 0))
            zs = [jnp.where(strict, z, NEG_INF) for z in zs]
        zs = [jnp.minimum(z, SB_MAX_LOG2_ODDS) for z in zs]
        sps = [jnp.log(1.0 + jnp.exp2(z)) * LOG2E for z in zs]
        rems = [jnp.dot(sps[hh].astype(jnp.bfloat16), tri, preferred_element_type=jnp.float32)
                + car_ref[hh, rows, :] for hh in heads]
        ws = [jnp.exp2((zs[hh] - sps[hh]) - rems[hh]).astype(jnp.bfloat16) for hh in heads]
        for hh in heads:
            acc_ref[hh, rows, :] += jnp.dot(ws[hh], v_ref[0, pl.ds(j0, t), cols[hh]],
                                            preferred_element_type=jnp.float32)
            car_ref[hh, rows, :] = rems[hh][:, 0:1] + sps[hh][:, 0:1]

    def q_block(i, _):
        q0 = pl.multiple_of(i * tq, tq)
        q1 = pl.multiple_of(i * tq + t, t)
        heads = range(SB_HEADS_PER_STEP)
        cols = [slice(hh * HEAD_DIM, (hh + 1) * HEAD_DIM) for hh in heads]
        q_i = [q_ref[0, pl.ds(q0, tq), c] for c in cols]
        acc_ref[...] = jnp.zeros(acc_ref.shape, jnp.float32)
        car_ref[...] = jnp.zeros(car_ref.shape, jnp.float32)

        def past_j0(n):
            return pl.multiple_of(jnp.maximum(2 * i - 1 - n, 0) * t, t)

        def stage_logits(z_ref, n):
            for hh, z in enumerate(logits(q_i, past_j0(n))):
                z_ref[hh] = z

        zs_low = logits([q_ref[0, pl.ds(q1, t), c] for c in cols], q1)
        zs_diag = logits(q_i, q0)
        stage_logits(za_ref, 0)
        tiles(zs_low, q1, slice(t, tq), True)
        tiles(zs_diag, q0, slice(0, tq), True)

        def past_pair(m, _):
            stage_logits(zb_ref, 2 * m + 1)
            tiles([za_ref[hh] for hh in heads], past_j0(2 * m), slice(0, tq), False)
            stage_logits(za_ref, 2 * m + 2)
            tiles([zb_ref[hh] for hh in heads], past_j0(2 * m + 1), slice(0, tq), False)
            return 0

        lax.fori_loop(0, i, past_pair, 0)
        acc = jnp.concatenate([acc_ref[hh] for hh in heads], axis=-1)
        o_ref[0, pl.ds(q0, tq), :] = (acc * _silu(g_ref[0, pl.ds(q0, tq), :])).astype(o_ref.dtype)
        return 0

    lax.fori_loop(0, nq, q_block, 0)


def _stick_breaking(qkv, g, heads):
    bsz, s, _ = qkv.shape
    assert s % (2 * SB_BLOCK) == 0 and heads % SB_HEADS_PER_STEP == 0
    steps = heads // SB_HEADS_PER_STEP
    width = SB_HEADS_PER_STEP * HEAD_DIM

    def col(group):
        return pl.BlockSpec((1, s, width), lambda h, b: (b, 0, h + group * steps))

    return pl.pallas_call(
        functools.partial(_sb_kernel, nq=s // (2 * SB_BLOCK)),
        out_shape=jax.ShapeDtypeStruct((bsz, s, heads * HEAD_DIM), jnp.bfloat16),
        grid=(steps, bsz),
        in_specs=[col(0), col(1), col(2), col(0)],
        out_specs=col(0),
        scratch_shapes=[pltpu.VMEM((SB_HEADS_PER_STEP, 2 * SB_BLOCK, SB_BLOCK), jnp.float32)] * 2
        + [pltpu.VMEM((SB_HEADS_PER_STEP, 2 * SB_BLOCK, HEAD_DIM), jnp.float32),
           pltpu.VMEM((SB_HEADS_PER_STEP, 2 * SB_BLOCK, 1), jnp.float32)],
        compiler_params=_cparams(2),
        name="stick_breaking",
    )(qkv, qkv, qkv, g)


def _xattn_kernel(q_ref, g_ref, k_ref, v_ref, o_ref, *, dh):
    scale = dh ** -0.5
    for h in range(XATTN_HEADS):
        cols = slice(h * dh, (h + 1) * dh)
        s = lax.dot_general(q_ref[0, :, cols], k_ref[0, :, cols], _NT,
                            preferred_element_type=jnp.float32) * scale
        m = jnp.max(s, axis=-1, keepdims=True)
        p = jnp.exp(s - m)
        l = jnp.sum(p, axis=-1, keepdims=True)
        o = jnp.dot(p.astype(jnp.bfloat16), v_ref[0, :, cols],
                    preferred_element_type=jnp.float32) / l
        o_ref[0, :, cols] = (o * _silu(g_ref[0, :, cols])).astype(o_ref.dtype)


def _xattn(q, g, kv, bm=512):
    bsz, s, w = q.shape
    n_mem = kv.shape[1]
    bm = min(bm, s)
    row = pl.BlockSpec((1, bm, w), lambda b, i: (b, i, 0))
    return pl.pallas_call(
        functools.partial(_xattn_kernel, dh=w // XATTN_HEADS),
        out_shape=jax.ShapeDtypeStruct((bsz, s, w), jnp.bfloat16),
        grid=(bsz, s // bm),
        in_specs=[row, row,
                  pl.BlockSpec((1, n_mem, w), lambda b, i: (b, 0, 0)),
                  pl.BlockSpec((1, n_mem, w), lambda b, i: (b, 0, 1))],
        out_specs=row,
        compiler_params=_cparams(2),
        name="xattn",
    )(q, g, kv, kv)


def _memory_block(x_f32, x_bf16, mem_bf16, w_xq, w_xkv, w_xo, g2, b2, bsz, s, emit_bf16):
    xw = w_xo.shape[0]
    q = _matmul(x_bf16, w_xq, 0, xw, jnp.bfloat16)
    g = _matmul(x_bf16, w_xq, xw, xw, jnp.float32)
    kv = _matmul(mem_bf16, w_xkv, 0, 2 * xw, jnp.bfloat16)
    n_mem = mem_bf16.shape[0] // bsz
    o = _xattn(q.reshape(bsz, s, xw), g.reshape(bsz, s, xw), kv.reshape(bsz, n_mem, 2 * xw))
    return _matmul_residual_ln(o.reshape(bsz * s, xw), w_xo, x_f32, g2, b2, emit_bf16)


def kernel(x, mem, rel_bias, w_in_0, conv_w_0, w_out_0, ln1_g_0, ln1_b_0, w_xq_0, w_xkv_0, w_xo_0, ln2_g_0, ln2_b_0, w_in_1, w_out_1, ln1_g_1, ln1_b_1, w_xq_1, w_xkv_1, w_xo_1, ln2_g_1, ln2_b_1):
    bsz, s, d = x.shape
    bf = jnp.bfloat16
    xf = x.reshape(bsz * s, d)
    xb = xf.astype(bf)
    memb = mem.reshape(-1, d).astype(bf)

    cw = conv_w_0.shape[1]
    mw = w_in_0.shape[1] // 4 - cw
    heads = mw // HEAD_DIM
    w_in = w_in_0.astype(bf)
    y_a = _conv_proj(xb, w_in, conv_w_0, cw, s)
    qkv = _matmul(xb, w_in, 4 * cw, 3 * mw, bf, scaled_cols=mw, scale=HEAD_DIM ** -0.5 * LOG2E)
    g_b = _matmul(xb, w_in, 4 * cw + 3 * mw, mw, jnp.float32)
    y_b = _moba(qkv.reshape(bsz, s, 3 * mw), g_b.reshape(bsz, s, mw), rel_bias.T, heads)
    xf, xb = _matmul_residual_ln_ktiled(y_a, cw, y_b.reshape(bsz * s, mw), 0, w_out_0.astype(bf),
                                        xf, ln1_g_0, ln1_b_0)
    xf, xb = _memory_block(xf, xb, memb, w_xq_0.astype(bf), w_xkv_0.astype(bf), w_xo_0.astype(bf),
                           ln2_g_0, ln2_b_0, bsz, s, True)

    sw = w_out_1.shape[0]
    w_in = w_in_1.astype(bf)
    qkv = _matmul(xb, w_in, 0, 3 * sw, bf, scaled_cols=sw, scale=HEAD_DIM ** -0.5 * LOG2E)
    g_c = _matmul(xb, w_in, 3 * sw, sw, jnp.float32)
    y_c = _stick_breaking(qkv.reshape(bsz, s, 3 * sw), g_c.reshape(bsz, s, sw), sw // HEAD_DIM)
    y_c = y_c.reshape(bsz * s, sw)
    xf, xb = _matmul_residual_ln_ktiled(y_c, sw // 2, y_c, sw // 2, w_out_1.astype(bf),
                                        xf, ln1_g_1, ln1_b_1)
    xf, _ = _memory_block(xf, xb, memb, w_xq_1.astype(bf), w_xkv_1.astype(bf), w_xo_1.astype(bf),
                          ln2_g_1, ln2_b_1, bsz, s, False)
    return xf.reshape(bsz, s, d)
```

```python
import functools
import math

import numpy as np
import jax
import jax.numpy as jnp
from jax import lax
from jax.experimental import pallas as pl
from jax.experimental.pallas import tpu as pltpu

DEPTH = 2
HEAD_DIM = 128
CONV_K = 3
MOBA_BLOCK = 256
MOBA_TOPK = 3
XATTN_HEADS = 4
REL_BUCKETS = 32
REL_MAX_DIST = 128
LN_EPS = 1e-5
NEG_INF = -1e30
DEEPNORM_ALPHA = (2 * DEPTH) ** 0.25
LOG2E = math.log2(math.e)

SB_BLOCK = 256
SB_HEADS_PER_STEP = 4
SB_MAX_LOG2_ODDS = 60.0
VMEM_LIMIT = 56 * 1024 * 1024

_NT = (((1,), (1,)), ((), ()))


def _cparams(n_axes):
    return pltpu.CompilerParams(dimension_semantics=("arbitrary",) * n_axes,
                                vmem_limit_bytes=VMEM_LIMIT)


def _silu(g):
    return g * jax.nn.sigmoid(g)


def _split_bf16(x):
    hi = x.astype(jnp.bfloat16)
    lo = (x - hi.astype(jnp.float32)).astype(jnp.bfloat16)
    return hi, lo


def _mm_kernel(a_ref, w_ref, o_ref, *, scaled_blocks, scale):
    acc = jnp.dot(a_ref[...], w_ref[...], preferred_element_type=jnp.float32)
    if scaled_blocks:
        acc = acc * jnp.where(pl.program_id(1) < scaled_blocks, scale, 1.0)
    o_ref[...] = acc.astype(o_ref.dtype)


def _matmul(a, w, n_start, n_size, out_dtype, bm=1024, bn=1024, scaled_cols=0, scale=1.0):
    m, k = a.shape
    bm = min(bm, m)
    bn = math.gcd(math.gcd(math.gcd(bn, n_size), n_start), scaled_cols)
    assert m % bm == 0 and bn % 128 == 0
    joff = n_start // bn
    return pl.pallas_call(
        functools.partial(_mm_kernel, scaled_blocks=scaled_cols // bn, scale=scale),
        out_shape=jax.ShapeDtypeStruct((m, n_size), out_dtype),
        grid=(m // bm, n_size // bn),
        in_specs=[pl.BlockSpec((bm, k), lambda i, j: (i, 0)),
                  pl.BlockSpec((k, bn), lambda i, j: (0, j + joff))],
        out_specs=pl.BlockSpec((bm, bn), lambda i, j: (i, j)),
        compiler_params=_cparams(2),
        name="matmul",
    )(a, w)


def _mm2_kernel(a1_ref, a2_ref, w_ref, o_ref):
    k1 = a1_ref.shape[1]
    acc = jnp.dot(a1_ref[...], w_ref[0:k1, :], preferred_element_type=jnp.float32)
    acc = acc + jnp.dot(a2_ref[...], w_ref[k1:, :], preferred_element_type=jnp.float32)
    o_ref[...] = acc.astype(o_ref.dtype)


def _matmul_cat(a1, a2, w, out_dtype, bm=1024, bn=1024):
    m, k1 = a1.shape
    k2 = a2.shape[1]
    n = w.shape[1]
    bm = min(bm, m)
    bn = min(bn, n)
    assert m % bm == 0 and n % bn == 0 and w.shape[0] == k1 + k2
    return pl.pallas_call(
        _mm2_kernel,
        out_shape=jax.ShapeDtypeStruct((m, n), out_dtype),
        grid=(m // bm, n // bn),
        in_specs=[pl.BlockSpec((bm, k1), lambda i, j: (i, 0)),
                  pl.BlockSpec((bm, k2), lambda i, j: (i, 0)),
                  pl.BlockSpec((k1 + k2, bn), lambda i, j: (0, j))],
        out_specs=pl.BlockSpec((bm, bn), lambda i, j: (i, j)),
        compiler_params=_cparams(2),
        name="matmul_cat",
    )(a1, a2, w)


def _layer_norm_rows(r, g, b):
    mu = jnp.mean(r, axis=-1, keepdims=True)
    d = r - mu
    var = jnp.mean(d * d, axis=-1, keepdims=True)
    return d * lax.rsqrt(var + LN_EPS) * g + b


def _mm_ln_pipelined_kernel(a_ref, w_ref, x_ref, g_ref, b_ref, of_ref, *rest):
    ob_ref = rest[0] if len(rest) == 3 else None
    y_refs = rest[-2:]
    i = pl.program_id(0)

    def multiply(y_ref):
        y_ref[...] = jnp.dot(a_ref[...], w_ref[...], preferred_element_type=jnp.float32)

    def normalise(y_ref):
        o = _layer_norm_rows(DEEPNORM_ALPHA * x_ref[...] + y_ref[...], g_ref[...], b_ref[...])
        of_ref[...] = o
        if ob_ref is not None:
            ob_ref[...] = o.astype(jnp.bfloat16)

    @pl.when(i == 0)
    def _():
        multiply(y_refs[0])

    for parity in (0, 1):
        @pl.when((i > 0) & (i % 2 == parity))
        def _():
            multiply(y_refs[parity])
            normalise(y_refs[1 - parity])


def _matmul_residual_ln(a, w, x, g, b, emit_bf16, bm=256):
    m, k = a.shape
    d = w.shape[1]
    bm = min(bm, m)
    n_tiles = m // bm
    lagged = pl.BlockSpec((bm, d), lambda i: (jnp.maximum(i - 1, 0), 0))
    vec = pl.BlockSpec((1, d), lambda i: (0, 0))
    out_shape = [jax.ShapeDtypeStruct((m, d), jnp.float32)]
    if emit_bf16:
        out_shape.append(jax.ShapeDtypeStruct((m, d), jnp.bfloat16))
    out = pl.pallas_call(
        _mm_ln_pipelined_kernel,
        out_shape=tuple(out_shape),
        grid=(n_tiles + 1,),
        in_specs=[pl.BlockSpec((bm, k), lambda i: (jnp.minimum(i, n_tiles - 1), 0)),
                  pl.BlockSpec((k, d), lambda i: (0, 0)), lagged, vec, vec],
        out_specs=tuple([lagged] * len(out_shape)),
        scratch_shapes=[pltpu.VMEM((bm, d), jnp.float32)] * 2,
        compiler_params=_cparams(1),
        name="matmul_residual_ln",
    )(a, w, x, g.reshape(1, d), b.reshape(1, d))
    return out if emit_bf16 else (out[0], None)


def _conv_proj_kernel(a_ref, wh_ref, wb_ref, wc_ref, wg_ref, cw_ref, o_ref, tail_ref, *,
                      tiles_per_seq):
    i = pl.program_id(0)
    j = pl.program_id(1)
    a = a_ref[...]
    u = (jnp.dot(a, wc_ref[...], preferred_element_type=jnp.float32)
         * jnp.dot(a, wh_ref[...], preferred_element_type=jnp.float32))
    bm = u.shape[0]

    @pl.when(i % tiles_per_seq == 0)
    def _():
        tail_ref[j] = jnp.zeros(tail_ref.shape[1:], jnp.float32)

    prev = tail_ref[j]
    tail_ref[j] = u[bm - 8:bm, :]
    row = lax.broadcasted_iota(jnp.int32, u.shape, 0)
    u1 = jnp.where(row == 0, prev[7:8, :], pltpu.roll(u, 1, 0))
    u2 = jnp.where(row == 0, prev[6:7, :],
                   jnp.where(row == 1, prev[7:8, :], pltpu.roll(u, 2, 0)))
    w = cw_ref[...]
    conv = w[0:1, :] * u2 + w[1:2, :] * u1 + w[2:3, :] * u
    b = jnp.dot(a, wb_ref[...], preferred_element_type=jnp.float32)
    g = jnp.dot(a, wg_ref[...], preferred_element_type=jnp.float32)
    o_ref[...] = (b * conv * _silu(g)).astype(o_ref.dtype)


def _conv_proj(a, w_in, conv_w, width, seq_len, bm=1024, bc=256):
    m, k = a.shape
    bm = min(bm, seq_len)
    bc = min(bc, width)
    nc = width // bc
    assert seq_len % bm == 0 and m % seq_len == 0 and width % bc == 0 and bm % 8 == 0

    def wcol(group):
        return pl.BlockSpec((k, bc), lambda i, j: (0, j + group * nc))

    return pl.pallas_call(
        functools.partial(_conv_proj_kernel, tiles_per_seq=seq_len // bm),
        out_shape=jax.ShapeDtypeStruct((m, width), jnp.bfloat16),
        grid=(m // bm, nc),
        in_specs=[pl.BlockSpec((bm, k), lambda i, j: (i, 0)),
                  wcol(0), wcol(1), wcol(2), wcol(3),
                  pl.BlockSpec((CONV_K, bc), lambda i, j: (0, j))],
        out_specs=pl.BlockSpec((bm, bc), lambda i, j: (i, j)),
        scratch_shapes=[pltpu.VMEM((nc, 8, bc), jnp.float32)],
        compiler_params=_cparams(2),
        name="conv_proj",
    )(a, w_in, w_in, w_in, w_in, conv_w)


def _t5_bucket_thresholds():
    d = np.arange(0, 4 * REL_MAX_DIST)
    max_exact = REL_BUCKETS // 2
    nf = np.maximum(d, 1).astype(np.float32)
    large = max_exact + (np.log(nf / np.float32(max_exact)) / np.float32(math.log(REL_MAX_DIST / max_exact))
                         * np.float32(REL_BUCKETS - max_exact)).astype(np.int32)
    large = np.minimum(large, REL_BUCKETS - 1)
    bucket = np.where(d < max_exact, d, large)
    assert np.all(np.diff(bucket) >= 0) and bucket[-1] == REL_BUCKETS - 1
    return [int(np.argmax(bucket >= b)) for b in range(REL_BUCKETS)]


_BUCKET_THR = _t5_bucket_thresholds()
_MOBA_BLOCK_SHIFT = MOBA_BLOCK.bit_length() - 1
assert 1 << _MOBA_BLOCK_SHIFT == MOBA_BLOCK


def _moba_kernel(rb_ref, q_ref, k_ref, v_ref, g_ref, o_ref, t0_ref, t1_ref, kaug_ref, *, nb):
    blk = MOBA_BLOCK
    h = pl.program_id(0)
    s_len = nb * blk
    qo = lax.broadcasted_iota(jnp.int32, (blk, blk), 0)
    ko = lax.broadcasted_iota(jnp.int32, (blk, blk), 1)

    @pl.when(pl.program_id(1) == 0)
    def _():
        def table(dist):
            t = jnp.full(dist.shape, rb_ref[h, REL_BUCKETS - 1], jnp.float32)
            for bkt in range(REL_BUCKETS - 2, -1, -1):
                t = jnp.where(dist < _BUCKET_THR[bkt + 1], rb_ref[h, bkt], t)
            return t * LOG2E
        t0_ref[...] = table(qo - ko)
        t1_ref[...] = table(qo - ko + blk)

    rb_far = rb_ref[h, REL_BUCKETS - 1] * LOG2E
    assert _BUCKET_THR[REL_BUCKETS - 1] <= blk + 1

    kaug_ref[:, 0:HEAD_DIM] = k_ref[0]
    kaug_ref[:, HEAD_DIM:2 * HEAD_DIM] = jnp.where(
        jnp.right_shift(lax.broadcasted_iota(jnp.int32, (s_len, HEAD_DIM), 0), _MOBA_BLOCK_SHIFT)
        == lax.broadcasted_iota(jnp.int32, (s_len, HEAD_DIM), 1), 1.0, 0.0).astype(jnp.bfloat16)

    kbar = jnp.concatenate(
        [jnp.mean(k_ref[0, j * blk:(j + 1) * blk, :].astype(jnp.float32), axis=0, keepdims=True)
         for j in range(nb)] + [jnp.zeros((16 - nb, HEAD_DIM), jnp.float32)], axis=0)
    kbar_hi, kbar_lo = _split_bf16(kbar)

    eye = jnp.where(qo == ko, 1.0, 0.0).astype(jnp.bfloat16)
    jidx = lax.broadcasted_iota(jnp.int32, (16, blk), 0)

    def logits(i):
        q_i = q_ref[0, i * blk:(i + 1) * blk, :]
        nk = (i + 1) * blk
        if i > MOBA_TOPK:
            gate = (lax.dot_general(kbar_hi, q_i, _NT, preferred_element_type=jnp.float32)
                    + lax.dot_general(kbar_lo, q_i, _NT, preferred_element_type=jnp.float32))
            cnt = jnp.zeros((16, blk), jnp.float32)
            for jp in range(i):
                row = gate[jp:jp + 1, :]
                beats = jnp.where(row > gate, 1.0,
                                  jnp.where(row == gate, jnp.where(jidx > jp, 1.0, 0.0), 0.0))
                cnt = cnt + beats
            drop_t = jnp.where(cnt < MOBA_TOPK, 0.0, jnp.where(jidx < i, NEG_INF, 0.0))
            drop_t = jnp.concatenate([drop_t, jnp.zeros((128 - 16, blk), jnp.float32)], axis=0)
            drop = lax.dot_general(eye, drop_t.astype(jnp.bfloat16), _NT,
                                   preferred_element_type=jnp.float32)
            q_aug = jnp.concatenate([q_i, drop.astype(jnp.bfloat16)], axis=-1)
            return lax.dot_general(q_aug, kaug_ref[0:nk, :], _NT,
                                   preferred_element_type=jnp.float32)
        return lax.dot_general(q_i, k_ref[0, 0:nk, :], _NT, preferred_element_type=jnp.float32)

    def attend(i, s):
        nk = (i + 1) * blk
        n_far = max(i - 1, 0) * blk
        lg_own = jnp.where(ko <= qo, s[:, i * blk:nk] + t0_ref[...], NEG_INF)
        m = jnp.max(lg_own, axis=-1, keepdims=True)
        if i > 0:
            lg_prev = s[:, (i - 1) * blk:i * blk] + t1_ref[...]
            m = jnp.maximum(m, jnp.max(lg_prev, axis=-1, keepdims=True))
        if n_far:
            m = jnp.maximum(m, jnp.max(s[:, 0:n_far], axis=-1, keepdims=True) + rb_far)
        pieces = [jnp.exp2(lg_own - m)]
        if i > 0:
            pieces.insert(0, jnp.exp2(lg_prev - m))
        if n_far:
            pieces.insert(0, jnp.exp2(s[:, 0:n_far] - (m - rb_far)))
        l = sum(jnp.sum(piece, axis=-1, keepdims=True) for piece in pieces)
        p = pieces[0] if len(pieces) == 1 else jnp.concatenate(pieces, axis=-1)
        o = jnp.dot(p.astype(jnp.bfloat16), v_ref[0, 0:nk, :],
                    preferred_element_type=jnp.float32) / l
        o_ref[0, i * blk:(i + 1) * blk, :] = (o * _silu(g_ref[0, i * blk:(i + 1) * blk, :])
                                              ).astype(o_ref.dtype)

    s_next = logits(0)
    for i in range(nb):
        s_cur, s_next = s_next, (logits(i + 1) if i + 1 < nb else None)
        attend(i, s_cur)


def _moba(qkv, g, rel_bias_t, heads):
    bsz, s, _ = qkv.shape
    assert s % MOBA_BLOCK == 0
    nb = s // MOBA_BLOCK
    assert MOBA_TOPK < nb <= 16

    def col(group):
        return pl.BlockSpec((1, s, HEAD_DIM), lambda h, b: (b, 0, h + group * heads))

    return pl.pallas_call(
        functools.partial(_moba_kernel, nb=nb),
        out_shape=jax.ShapeDtypeStruct((bsz, s, heads * HEAD_DIM), jnp.bfloat16),
        grid=(heads, bsz),
        in_specs=[pl.BlockSpec(memory_space=pltpu.SMEM), col(0), col(1), col(2),
                  pl.BlockSpec((1, s, HEAD_DIM), lambda h, b: (b, 0, h))],
        out_specs=pl.BlockSpec((1, s, HEAD_DIM), lambda h, b: (b, 0, h)),
        scratch_shapes=[pltpu.VMEM((MOBA_BLOCK, MOBA_BLOCK), jnp.float32),
                        pltpu.VMEM((MOBA_BLOCK, MOBA_BLOCK), jnp.float32),
                        pltpu.VMEM((s, 2 * HEAD_DIM), jnp.bfloat16)],
        compiler_params=_cparams(2),
        name="moba",
    )(rel_bias_t, qkv, qkv, qkv, g)


def _sb_kernel(q_ref, k_ref, v_ref, g_ref, o_ref, za_ref, zb_ref, acc_ref, car_ref, *, nq):
    t = SB_BLOCK
    tq = 2 * t
    tri = jnp.where(lax.broadcasted_iota(jnp.int32, (t, t), 0)
                    > lax.broadcasted_iota(jnp.int32, (t, t), 1), 1.0, 0.0).astype(jnp.bfloat16)

    def logits(qs, j0):
        return tuple(lax.dot_general(q, k_ref[0, pl.ds(j0, t), hh * HEAD_DIM:(hh + 1) * HEAD_DIM],
                                     _NT, preferred_element_type=jnp.float32)
                     for hh, q in enumerate(qs))

    def tiles(zs, j0, rows, diag):
        heads = range(len(zs))
        cols = [slice(hh * HEAD_DIM, (hh + 1) * HEAD_DIM) for hh in heads]
        if diag:
            strict = (lax.broadcasted_iota(jnp.int32, zs[0].shape, 1)
                      < lax.broadcasted_iota(jnp.int32, zs[0].shape, 0))
            zs = [jnp.where(strict, z, NEG_INF) for z in zs]
        zs = [jnp.minimum(z, SB_MAX_LOG2_ODDS) for z in zs]
        sps = [jnp.log(1.0 + jnp.exp2(z)) * LOG2E for z in zs]
        rems = [jnp.dot(sps[hh].astype(jnp.bfloat16), tri, preferred_element_type=jnp.float32)
                + car_ref[hh, rows, :] for hh in heads]
        ws = [jnp.exp2((zs[hh] - sps[hh]) - rems[hh]).astype(jnp.bfloat16) for hh in heads]
        for hh in heads:
            acc_ref[hh, rows, :] += jnp.dot(ws[hh], v_ref[0, pl.ds(j0, t), cols[hh]],
                                            preferred_element_type=jnp.float32)
            car_ref[hh, rows, :] = rems[hh][:, 0:1] + sps[hh][:, 0:1]

    def q_block(i, _):
        q0 = pl.multiple_of(i * tq, tq)
        q1 = pl.multiple_of(i * tq + t, t)
        heads = range(SB_HEADS_PER_STEP)
        cols = [slice(hh * HEAD_DIM, (hh + 1) * HEAD_DIM) for hh in heads]
        q_i = [q_ref[0, pl.ds(q0, tq), c] for c in cols]
        acc_ref[...] = jnp.zeros(acc_ref.shape, jnp.float32)
        car_ref[...] = jnp.zeros(car_ref.shape, jnp.float32)

        def past_j0(n):
            return pl.multiple_of(jnp.maximum(2 * i - 1 - n, 0) * t, t)

        def stage_logits(z_ref, n):
            for hh, z in enumerate(logits(q_i, past_j0(n))):
                z_ref[hh] = z

        zs_low = logits([q_ref[0, pl.ds(q1, t), c] for c in cols], q1)
        zs_diag = logits(q_i, q0)
        stage_logits(za_ref, 0)
        tiles(zs_low, q1, slice(t, tq), True)
        tiles(zs_diag, q0, slice(0, tq), True)

        def past_pair(m, _):
            stage_logits(zb_ref, 2 * m + 1)
            tiles([za_ref[hh] for hh in heads], past_j0(2 * m), slice(0, tq), False)
            stage_logits(za_ref, 2 * m + 2)
            tiles([zb_ref[hh] for hh in heads], past_j0(2 * m + 1), slice(0, tq), False)
            return 0

        lax.fori_loop(0, i, past_pair, 0)
        acc = jnp.concatenate([acc_ref[hh] for hh in heads], axis=-1)
        o_ref[0, pl.ds(q0, tq), :] = (acc * _silu(g_ref[0, pl.ds(q0, tq), :])).astype(o_ref.dtype)
        return 0

    lax.fori_loop(0, nq, q_block, 0)


def _stick_breaking(qkv, g, heads):
    bsz, s, _ = qkv.shape
    assert s % (2 * SB_BLOCK) == 0 and heads % SB_HEADS_PER_STEP == 0
    steps = heads // SB_HEADS_PER_STEP
    width = SB_HEADS_PER_STEP * HEAD_DIM

    def col(group):
        return pl.BlockSpec((1, s, width), lambda h, b: (b, 0, h + group * steps))

    return pl.pallas_call(
        functools.partial(_sb_kernel, nq=s // (2 * SB_BLOCK)),
        out_shape=jax.ShapeDtypeStruct((bsz, s, heads * HEAD_DIM), jnp.bfloat16),
        grid=(steps, bsz),
        in_specs=[col(0), col(1), col(2), col(0)],
        out_specs=col(0),
        scratch_shapes=[pltpu.VMEM((SB_HEADS_PER_STEP, 2 * SB_BLOCK, SB_BLOCK), jnp.float32)] * 2
        + [pltpu.VMEM((SB_HEADS_PER_STEP, 2 * SB_BLOCK, HEAD_DIM), jnp.float32),
           pltpu.VMEM((SB_HEADS_PER_STEP, 2 * SB_BLOCK, 1), jnp.float32)],
        compiler_params=_cparams(2),
        name="stick_breaking",
    )(qkv, qkv, qkv, g)


def _xattn_kernel(q_ref, g_ref, k_ref, v_ref, o_ref, *, dh):
    scale = dh ** -0.5
    for h in range(XATTN_HEADS):
        cols = slice(h * dh, (h + 1) * dh)
        s = lax.dot_general(q_ref[0, :, cols], k_ref[0, :, cols], _NT,
                            preferred_element_type=jnp.float32) * scale
        m = jnp.max(s, axis=-1, keepdims=True)
        p = jnp.exp(s - m)
        l = jnp.sum(p, axis=-1, keepdims=True)
        o = jnp.dot(p.astype(jnp.bfloat16), v_ref[0, :, cols],
                    preferred_element_type=jnp.float32) / l
        o_ref[0, :, cols] = (o * _silu(g_ref[0, :, cols])).astype(o_ref.dtype)


def _xattn(q, g, kv, bm=512):
    bsz, s, w = q.shape
    n_mem = kv.shape[1]
    bm = min(bm, s)
    row = pl.BlockSpec((1, bm, w), lambda b, i: (b, i, 0))
    return pl.pallas_call(
        functools.partial(_xattn_kernel, dh=w // XATTN_HEADS),
        out_shape=jax.ShapeDtypeStruct((bsz, s, w), jnp.bfloat16),
        grid=(bsz, s // bm),
        in_specs=[row, row,
                  pl.BlockSpec((1, n_mem, w), lambda b, i: (b, 0, 0)),
                  pl.BlockSpec((1, n_mem, w), lambda b, i: (b, 0, 1))],
        out_specs=row,
        compiler_params=_cparams(2),
        name="xattn",
    )(q, g, kv, kv)


def _ln_proj_kernel(x_ref, y_ref, g_ref, b_ref, w_ref, xf_ref, q_ref, gate_ref, a0_ref, a1_ref, *,
                    q_blocks):
    i = pl.program_id(0)
    j = pl.program_id(1)
    chunk = x_ref.shape[0]
    a_refs = (a0_ref, a1_ref)

    def normalise(a_ref):
        o = _layer_norm_rows(DEEPNORM_ALPHA * x_ref[...] + y_ref[...], g_ref[...], b_ref[...])
        xf_ref[...] = o
        a_ref[pl.ds(pl.multiple_of(j * chunk, chunk), chunk), :] = o.astype(jnp.bfloat16)

    def multiply(a_ref):
        acc = jnp.dot(a_ref[...], w_ref[...], preferred_element_type=jnp.float32)

        @pl.when(j < q_blocks)
        def _():
            q_ref[...] = acc.astype(q_ref.dtype)

        @pl.when(j >= q_blocks)
        def _():
            gate_ref[...] = acc

    @pl.when(i == 0)
    def _():
        normalise(a_refs[0])

    for parity in (0, 1):
        @pl.when((i > 0) & (i % 2 == parity))
        def _():
            normalise(a_refs[parity])
            multiply(a_refs[1 - parity])


def _residual_ln_proj(x, y, g, b, w, q_cols, bm=1024, bn=256):
    m, d = x.shape
    n = w.shape[1]
    bm = min(bm, m)
    bn = math.gcd(bn, q_cols, n - q_cols)
    nj = n // bn
    chunk = bm // nj
    n_tiles = m // bm
    q_blocks = q_cols // bn
    assert m % bm == 0 and bm % nj == 0 and chunk % 16 == 0 and 0 < q_blocks < nj

    def chunk_idx(i, j):
        return (jnp.where(i < n_tiles, i * nj + j, n_tiles * nj - 1), 0)

    def out_idx(col):
        return lambda i, j: (jnp.maximum(i - 1, 0), jnp.where(i == 0, 0, col(j)))

    rows = pl.BlockSpec((chunk, d), chunk_idx)
    vec = pl.BlockSpec((1, d), lambda i, j: (0, 0))
    return pl.pallas_call(
        functools.partial(_ln_proj_kernel, q_blocks=q_blocks),
        out_shape=(jax.ShapeDtypeStruct((m, d), jnp.float32),
                   jax.ShapeDtypeStruct((m, q_cols), jnp.bfloat16),
                   jax.ShapeDtypeStruct((m, n - q_cols), jnp.float32)),
        grid=(n_tiles + 1, nj),
        in_specs=[rows, rows, vec, vec, pl.BlockSpec((d, bn), lambda i, j: (0, j))],
        out_specs=(rows,
                   pl.BlockSpec((bm, bn), out_idx(lambda j: jnp.minimum(j, q_blocks - 1))),
                   pl.BlockSpec((bm, bn), out_idx(lambda j: jnp.maximum(j - q_blocks, 0)))),
        scratch_shapes=[pltpu.VMEM((bm, d), jnp.bfloat16)] * 2,
        compiler_params=_cparams(2),
        name="residual_ln_proj",
    )(x, y, g.reshape(1, d), b.reshape(1, d), w)


def _memory_block(x_f32, y, g1, b1, mem_bf16, w_xq, w_xkv, w_xo, g2, b2, bsz, s, emit_bf16):
    xw = w_xo.shape[0]
    x_f32, q, g = _residual_ln_proj(x_f32, y, g1, b1, w_xq, xw)
    kv = _matmul(mem_bf16, w_xkv, 0, 2 * xw, jnp.bfloat16)
    n_mem = mem_bf16.shape[0] // bsz
    o = _xattn(q.reshape(bsz, s, xw), g.reshape(bsz, s, xw), kv.reshape(bsz, n_mem, 2 * xw))
    return _matmul_residual_ln(o.reshape(bsz * s, xw), w_xo, x_f32, g2, b2, emit_bf16)


def kernel(x, mem, rel_bias, w_in_0, conv_w_0, w_out_0, ln1_g_0, ln1_b_0, w_xq_0, w_xkv_0, w_xo_0, ln2_g_0, ln2_b_0, w_in_1, w_out_1, ln1_g_1, ln1_b_1, w_xq_1, w_xkv_1, w_xo_1, ln2_g_1, ln2_b_1):
    bsz, s, d = x.shape
    bf = jnp.bfloat16
    xf = x.reshape(bsz * s, d)
    xb = xf.astype(bf)
    memb = mem.reshape(-1, d).astype(bf)

    cw = conv_w_0.shape[1]
    mw = w_in_0.shape[1] // 4 - cw
    heads = mw // HEAD_DIM
    w_in = w_in_0.astype(bf)
    y_a = _conv_proj(xb, w_in, conv_w_0, cw, s)
    qkv = _matmul(xb, w_in, 4 * cw, 3 * mw, bf, scaled_cols=mw, scale=HEAD_DIM ** -0.5 * LOG2E)
    g_b = _matmul(xb, w_in, 4 * cw + 3 * mw, mw, jnp.float32)
    y_b = _moba(qkv.reshape(bsz, s, 3 * mw), g_b.reshape(bsz, s, mw), rel_bias.T, heads)
    y = _matmul_cat(y_a, y_b.reshape(bsz * s, mw), w_out_0.astype(bf), jnp.float32)
    xf, xb = _memory_block(xf, y, ln1_g_0, ln1_b_0, memb, w_xq_0.astype(bf), w_xkv_0.astype(bf),
                           w_xo_0.astype(bf), ln2_g_0, ln2_b_0, bsz, s, True)

    sw = w_out_1.shape[0]
    w_in = w_in_1.astype(bf)
    qkv = _matmul(xb, w_in, 0, 3 * sw, bf, scaled_cols=sw, scale=HEAD_DIM ** -0.5 * LOG2E)
    g_c = _matmul(xb, w_in, 3 * sw, sw, jnp.float32)
    y_c = _stick_breaking(qkv.reshape(bsz, s, 3 * sw), g_c.reshape(bsz, s, sw), sw // HEAD_DIM)
    y = _matmul(y_c.reshape(bsz * s, sw), w_out_1.astype(bf), 0, d, jnp.float32)
    xf, _ = _memory_block(xf, y, ln1_g_1, ln1_b_1, memb, w_xq_1.astype(bf), w_xkv_1.astype(bf),
                          w_xo_1.astype(bf), ln2_g_1, ln2_b_1, bsz, s, False)
    return xf.reshape(bsz, s, d)
```

```python
import functools
import math

import numpy as np
import jax
import jax.numpy as jnp
from jax import lax
from jax.experimental import pallas as pl
from jax.experimental.pallas import tpu as pltpu

DEPTH = 2
HEAD_DIM = 128
CONV_K = 3
MOBA_BLOCK = 256
MOBA_TOPK = 3
XATTN_HEADS = 4
REL_BUCKETS = 32
REL_MAX_DIST = 128
LN_EPS = 1e-5
NEG_INF = -1e30
DEEPNORM_ALPHA = (2 * DEPTH) ** 0.25
LOG2E = math.log2(math.e)

MOBA_HEADS_PER_STEP = 2
SB_BLOCK = 256
SB_HEADS_PER_STEP = 4
SB_MAX_LOG2_ODDS = 60.0
VMEM_LIMIT = 56 * 1024 * 1024

_NT = (((1,), (1,)), ((), ()))


def _cparams(n_axes):
    return pltpu.CompilerParams(dimension_semantics=("arbitrary",) * n_axes,
                                vmem_limit_bytes=VMEM_LIMIT)


def _silu(g):
    return g * jax.nn.sigmoid(g)


def _split_bf16(x):
    hi = x.astype(jnp.bfloat16)
    lo = (x - hi.astype(jnp.float32)).astype(jnp.bfloat16)
    return hi, lo


def _mm_kernel(a_ref, w_ref, o_ref, *, scaled_blocks, scale):
    acc = jnp.dot(a_ref[...], w_ref[...], preferred_element_type=jnp.float32)
    if scaled_blocks:
        acc = acc * jnp.where(pl.program_id(1) < scaled_blocks, scale, 1.0)
    o_ref[...] = acc.astype(o_ref.dtype)


def _matmul(a, w, n_start, n_size, out_dtype, bm=1024, bn=1024, scaled_cols=0, scale=1.0):
    m, k = a.shape
    bm = min(bm, m)
    bn = math.gcd(math.gcd(math.gcd(bn, n_size), n_start), scaled_cols)
    assert m % bm == 0 and bn % 128 == 0
    joff = n_start // bn
    return pl.pallas_call(
        functools.partial(_mm_kernel, scaled_blocks=scaled_cols // bn, scale=scale),
        out_shape=jax.ShapeDtypeStruct((m, n_size), out_dtype),
        grid=(m // bm, n_size // bn),
        in_specs=[pl.BlockSpec((bm, k), lambda i, j: (i, 0)),
                  pl.BlockSpec((k, bn), lambda i, j: (0, j + joff))],
        out_specs=pl.BlockSpec((bm, bn), lambda i, j: (i, j)),
        compiler_params=_cparams(2),
        name="matmul",
    )(a, w)


def _mm2_kernel(a1_ref, a2_ref, w_ref, o_ref):
    k1 = a1_ref.shape[1]
    acc = jnp.dot(a1_ref[...], w_ref[0:k1, :], preferred_element_type=jnp.float32)
    acc = acc + jnp.dot(a2_ref[...], w_ref[k1:, :], preferred_element_type=jnp.float32)
    o_ref[...] = acc.astype(o_ref.dtype)


def _matmul_cat(a1, a2, w, out_dtype, bm=1024, bn=1024):
    m, k1 = a1.shape
    k2 = a2.shape[1]
    n = w.shape[1]
    bm = min(bm, m)
    bn = min(bn, n)
    assert m % bm == 0 and n % bn == 0 and w.shape[0] == k1 + k2
    return pl.pallas_call(
        _mm2_kernel,
        out_shape=jax.ShapeDtypeStruct((m, n), out_dtype),
        grid=(m // bm, n // bn),
        in_specs=[pl.BlockSpec((bm, k1), lambda i, j: (i, 0)),
                  pl.BlockSpec((bm, k2), lambda i, j: (i, 0)),
                  pl.BlockSpec((k1 + k2, bn), lambda i, j: (0, j))],
        out_specs=pl.BlockSpec((bm, bn), lambda i, j: (i, j)),
        compiler_params=_cparams(2),
        name="matmul_cat",
    )(a1, a2, w)


def _layer_norm_rows(r, g, b):
    mu = jnp.mean(r, axis=-1, keepdims=True)
    d = r - mu
    var = jnp.mean(d * d, axis=-1, keepdims=True)
    return d * lax.rsqrt(var + LN_EPS) * g + b


def _mm_ln_pipelined_kernel(a_ref, w_ref, x_ref, g_ref, b_ref, of_ref, *rest):
    ob_ref = rest[0] if len(rest) == 3 else None
    y_refs = rest[-2:]
    i = pl.program_id(0)

    def multiply(y_ref):
        y_ref[...] = jnp.dot(a_ref[...], w_ref[...], preferred_element_type=jnp.float32)

    def normalise(y_ref):
        o = _layer_norm_rows(DEEPNORM_ALPHA * x_ref[...] + y_ref[...], g_ref[...], b_ref[...])
        of_ref[...] = o
        if ob_ref is not None:
            ob_ref[...] = o.astype(jnp.bfloat16)

    @pl.when(i == 0)
    def _():
        multiply(y_refs[0])

    for parity in (0, 1):
        @pl.when((i > 0) & (i % 2 == parity))
        def _():
            multiply(y_refs[parity])
            normalise(y_refs[1 - parity])


def _matmul_residual_ln(a, w, x, g, b, emit_bf16, bm=256):
    m, k = a.shape
    d = w.shape[1]
    bm = min(bm, m)
    n_tiles = m // bm
    lagged = pl.BlockSpec((bm, d), lambda i: (jnp.maximum(i - 1, 0), 0))
    vec = pl.BlockSpec((1, d), lambda i: (0, 0))
    out_shape = [jax.ShapeDtypeStruct((m, d), jnp.float32)]
    if emit_bf16:
        out_shape.append(jax.ShapeDtypeStruct((m, d), jnp.bfloat16))
    out = pl.pallas_call(
        _mm_ln_pipelined_kernel,
        out_shape=tuple(out_shape),
        grid=(n_tiles + 1,),
        in_specs=[pl.BlockSpec((bm, k), lambda i: (jnp.minimum(i, n_tiles - 1), 0)),
                  pl.BlockSpec((k, d), lambda i: (0, 0)), lagged, vec, vec],
        out_specs=tuple([lagged] * len(out_shape)),
        scratch_shapes=[pltpu.VMEM((bm, d), jnp.float32)] * 2,
        compiler_params=_cparams(1),
        name="matmul_residual_ln",
    )(a, w, x, g.reshape(1, d), b.reshape(1, d))
    return out if emit_bf16 else (out[0], None)


def _conv_proj_kernel(a_ref, wh_ref, wb_ref, wc_ref, wg_ref, cw_ref, o_ref, tail_ref, *,
                      tiles_per_seq):
    i = pl.program_id(0)
    j = pl.program_id(1)
    a = a_ref[...]
    u = (jnp.dot(a, wc_ref[...], preferred_element_type=jnp.float32)
         * jnp.dot(a, wh_ref[...], preferred_element_type=jnp.float32))
    bm = u.shape[0]

    @pl.when(i % tiles_per_seq == 0)
    def _():
        tail_ref[j] = jnp.zeros(tail_ref.shape[1:], jnp.float32)

    prev = tail_ref[j]
    tail_ref[j] = u[bm - 8:bm, :]
    row = lax.broadcasted_iota(jnp.int32, u.shape, 0)
    u1 = jnp.where(row == 0, prev[7:8, :], pltpu.roll(u, 1, 0))
    u2 = jnp.where(row == 0, prev[6:7, :],
                   jnp.where(row == 1, prev[7:8, :], pltpu.roll(u, 2, 0)))
    w = cw_ref[...]
    conv = w[0:1, :] * u2 + w[1:2, :] * u1 + w[2:3, :] * u
    b = jnp.dot(a, wb_ref[...], preferred_element_type=jnp.float32)
    g = jnp.dot(a, wg_ref[...], preferred_element_type=jnp.float32)
    o_ref[...] = (b * conv * _silu(g)).astype(o_ref.dtype)


def _conv_proj(a, w_in, conv_w, width, seq_len, bm=1024, bc=256):
    m, k = a.shape
    bm = min(bm, seq_len)
    bc = min(bc, width)
    nc = width // bc
    assert seq_len % bm == 0 and m % seq_len == 0 and width % bc == 0 and bm % 8 == 0

    def wcol(group):
        return pl.BlockSpec((k, bc), lambda i, j: (0, j + group * nc))

    return pl.pallas_call(
        functools.partial(_conv_proj_kernel, tiles_per_seq=seq_len // bm),
        out_shape=jax.ShapeDtypeStruct((m, width), jnp.bfloat16),
        grid=(m // bm, nc),
        in_specs=[pl.BlockSpec((bm, k), lambda i, j: (i, 0)),
                  wcol(0), wcol(1), wcol(2), wcol(3),
                  pl.BlockSpec((CONV_K, bc), lambda i, j: (0, j))],
        out_specs=pl.BlockSpec((bm, bc), lambda i, j: (i, j)),
        scratch_shapes=[pltpu.VMEM((nc, 8, bc), jnp.float32)],
        compiler_params=_cparams(2),
        name="conv_proj",
    )(a, w_in, w_in, w_in, w_in, conv_w)


def _t5_bucket_thresholds():
    d = np.arange(0, 4 * REL_MAX_DIST)
    max_exact = REL_BUCKETS // 2
    nf = np.maximum(d, 1).astype(np.float32)
    large = max_exact + (np.log(nf / np.float32(max_exact)) / np.float32(math.log(REL_MAX_DIST / max_exact))
                         * np.float32(REL_BUCKETS - max_exact)).astype(np.int32)
    large = np.minimum(large, REL_BUCKETS - 1)
    bucket = np.where(d < max_exact, d, large)
    assert np.all(np.diff(bucket) >= 0) and bucket[-1] == REL_BUCKETS - 1
    return [int(np.argmax(bucket >= b)) for b in range(REL_BUCKETS)]


_BUCKET_THR = _t5_bucket_thresholds()
_MOBA_BLOCK_SHIFT = MOBA_BLOCK.bit_length() - 1
assert 1 << _MOBA_BLOCK_SHIFT == MOBA_BLOCK


def _moba_kernel(rb_ref, q_ref, k_ref, v_ref, g_ref, o_ref, t0_ref, t1_ref, kaug_ref, *, nb):
    blk = MOBA_BLOCK
    s_len = nb * blk
    heads = range(MOBA_HEADS_PER_STEP)
    head_ids = [pl.program_id(0) * MOBA_HEADS_PER_STEP + hh for hh in heads]
    cols = [slice(hh * HEAD_DIM, (hh + 1) * HEAD_DIM) for hh in heads]
    qo = lax.broadcasted_iota(jnp.int32, (blk, blk), 0)
    ko = lax.broadcasted_iota(jnp.int32, (blk, blk), 1)

    @pl.when(pl.program_id(1) == 0)
    def _():
        def table(h, dist):
            t = jnp.full(dist.shape, rb_ref[h, REL_BUCKETS - 1], jnp.float32)
            for bkt in range(REL_BUCKETS - 2, -1, -1):
                t = jnp.where(dist < _BUCKET_THR[bkt + 1], rb_ref[h, bkt], t)
            return t * LOG2E
        for hh in heads:
            t0_ref[hh] = table(head_ids[hh], qo - ko)
            t1_ref[hh] = table(head_ids[hh], qo - ko + blk)

    rb_far = [rb_ref[h, REL_BUCKETS - 1] * LOG2E for h in head_ids]
    assert _BUCKET_THR[REL_BUCKETS - 1] <= blk + 1

    block_onehot = jnp.where(
        jnp.right_shift(lax.broadcasted_iota(jnp.int32, (s_len, HEAD_DIM), 0), _MOBA_BLOCK_SHIFT)
        == lax.broadcasted_iota(jnp.int32, (s_len, HEAD_DIM), 1), 1.0, 0.0).astype(jnp.bfloat16)
    kbars = []
    for hh in heads:
        kaug_ref[hh, :, 0:HEAD_DIM] = k_ref[0, :, cols[hh]]
        kaug_ref[hh, :, HEAD_DIM:2 * HEAD_DIM] = block_onehot
        kbar = jnp.concatenate(
            [jnp.mean(k_ref[0, j * blk:(j + 1) * blk, cols[hh]].astype(jnp.float32),
                      axis=0, keepdims=True) for j in range(nb)]
            + [jnp.zeros((16 - nb, HEAD_DIM), jnp.float32)], axis=0)
        kbars.append(_split_bf16(kbar))

    eye = jnp.where(qo == ko, 1.0, 0.0).astype(jnp.bfloat16)
    jidx = lax.broadcasted_iota(jnp.int32, (16, blk), 0)

    def logits(hh, i):
        q_i = q_ref[0, i * blk:(i + 1) * blk, cols[hh]]
        kbar_hi, kbar_lo = kbars[hh]
        nk = (i + 1) * blk
        if i > MOBA_TOPK:
            gate = (lax.dot_general(kbar_hi, q_i, _NT, preferred_element_type=jnp.float32)
                    + lax.dot_general(kbar_lo, q_i, _NT, preferred_element_type=jnp.float32))
            cnt = jnp.zeros((16, blk), jnp.float32)
            for jp in range(i):
                row = gate[jp:jp + 1, :]
                beats = jnp.where(row > gate, 1.0,
                                  jnp.where(row == gate, jnp.where(jidx > jp, 1.0, 0.0), 0.0))
                cnt = cnt + beats
            drop_t = jnp.where(cnt < MOBA_TOPK, 0.0, jnp.where(jidx < i, NEG_INF, 0.0))
            drop_t = jnp.concatenate([drop_t, jnp.zeros((128 - 16, blk), jnp.float32)], axis=0)
            drop = lax.dot_general(eye, drop_t.astype(jnp.bfloat16), _NT,
                                   preferred_element_type=jnp.float32)
            q_aug = jnp.concatenate([q_i, drop.astype(jnp.bfloat16)], axis=-1)
            return lax.dot_general(q_aug, kaug_ref[hh, 0:nk, :], _NT,
                                   preferred_element_type=jnp.float32)
        return lax.dot_general(q_i, k_ref[0, 0:nk, cols[hh]], _NT,
                               preferred_element_type=jnp.float32)

    def attend(hh, i, s):
        nk = (i + 1) * blk
        n_far = max(i - 1, 0) * blk
        lg_own = jnp.where(ko <= qo, s[:, i * blk:nk] + t0_ref[hh], NEG_INF)
        m = jnp.max(lg_own, axis=-1, keepdims=True)
        if i > 0:
            lg_prev = s[:, (i - 1) * blk:i * blk] + t1_ref[hh]
            m = jnp.maximum(m, jnp.max(lg_prev, axis=-1, keepdims=True))
        if n_far:
            m = jnp.maximum(m, jnp.max(s[:, 0:n_far], axis=-1, keepdims=True) + rb_far[hh])
        pieces = [jnp.exp2(lg_own - m)]
        if i > 0:
            pieces.insert(0, jnp.exp2(lg_prev - m))
        if n_far:
            pieces.insert(0, jnp.exp2(s[:, 0:n_far] - (m - rb_far[hh])))
        l = sum(jnp.sum(piece, axis=-1, keepdims=True) for piece in pieces)
        p = pieces[0] if len(pieces) == 1 else jnp.concatenate(pieces, axis=-1)
        o = jnp.dot(p.astype(jnp.bfloat16), v_ref[0, 0:nk, cols[hh]],
                    preferred_element_type=jnp.float32) / l
        rows = slice(i * blk, (i + 1) * blk)
        o_ref[0, rows, cols[hh]] = (o * _silu(g_ref[0, rows, cols[hh]])).astype(o_ref.dtype)

    items = [(hh, i) for hh in heads for i in range(nb)]
    s_next = logits(*items[0])
    for n, item in enumerate(items):
        s_cur, s_next = s_next, (logits(*items[n + 1]) if n + 1 < len(items) else None)
        attend(*item, s_cur)


def _moba(qkv, g, rel_bias_t, heads):
    bsz, s, _ = qkv.shape
    assert s % MOBA_BLOCK == 0
    nb = s // MOBA_BLOCK
    assert MOBA_TOPK < nb <= 16 and heads % MOBA_HEADS_PER_STEP == 0
    steps = heads // MOBA_HEADS_PER_STEP
    width = MOBA_HEADS_PER_STEP * HEAD_DIM

    def col(group):
        return pl.BlockSpec((1, s, width), lambda h, b: (b, 0, h + group * steps))

    return pl.pallas_call(
        functools.partial(_moba_kernel, nb=nb),
        out_shape=jax.ShapeDtypeStruct((bsz, s, heads * HEAD_DIM), jnp.bfloat16),
        grid=(steps, bsz),
        in_specs=[pl.BlockSpec(memory_space=pltpu.SMEM), col(0), col(1), col(2), col(0)],
        out_specs=col(0),
        scratch_shapes=[pltpu.VMEM((MOBA_HEADS_PER_STEP, MOBA_BLOCK, MOBA_BLOCK), jnp.float32),
                        pltpu.VMEM((MOBA_HEADS_PER_STEP, MOBA_BLOCK, MOBA_BLOCK), jnp.float32),
                        pltpu.VMEM((MOBA_HEADS_PER_STEP, s, 2 * HEAD_DIM), jnp.bfloat16)],
        compiler_params=_cparams(2),
        name="moba",
    )(rel_bias_t, qkv, qkv, qkv, g)


def _sb_kernel(q_ref, k_ref, v_ref, g_ref, o_ref, za_ref, zb_ref, acc_ref, car_ref, *, nq):
    t = SB_BLOCK
    tq = 2 * t
    tri = jnp.where(lax.broadcasted_iota(jnp.int32, (t, t), 0)
                    > lax.broadcasted_iota(jnp.int32, (t, t), 1), 1.0, 0.0).astype(jnp.bfloat16)

    def logits(qs, j0):
        return tuple(lax.dot_general(q, k_ref[0, pl.ds(j0, t), hh * HEAD_DIM:(hh + 1) * HEAD_DIM],
                                     _NT, preferred_element_type=jnp.float32)
                     for hh, q in enumerate(qs))

    def tiles(zs, j0, rows, diag):
        heads = range(len(zs))
        cols = [slice(hh * HEAD_DIM, (hh + 1) * HEAD_DIM) for hh in heads]
        if diag:
            strict = (lax.broadcasted_iota(jnp.int32, zs[0].shape, 1)
                      < lax.broadcasted_iota(jnp.int32, zs[0].shape, 0))
            zs = [jnp.where(strict, z, NEG_INF) for z in zs]
        zs = [jnp.minimum(z, SB_MAX_LOG2_ODDS) for z in zs]
        sps = [jnp.log(1.0 + jnp.exp2(z)) * LOG2E for z in zs]
        rems = [jnp.dot(sps[hh].astype(jnp.bfloat16), tri, preferred_element_type=jnp.float32)
                + car_ref[hh, rows, :] for hh in heads]
        ws = [jnp.exp2((zs[hh] - sps[hh]) - rems[hh]).astype(jnp.bfloat16) for hh in heads]
        for hh in heads:
            acc_ref[hh, rows, :] += jnp.dot(ws[hh], v_ref[0, pl.ds(j0, t), cols[hh]],
                                            preferred_element_type=jnp.float32)
            car_ref[hh, rows, :] = rems[hh][:, 0:1] + sps[hh][:, 0:1]

    def q_block(i, _):
        q0 = pl.multiple_of(i * tq, tq)
        q1 = pl.multiple_of(i * tq + t, t)
        heads = range(SB_HEADS_PER_STEP)
        cols = [slice(hh * HEAD_DIM, (hh + 1) * HEAD_DIM) for hh in heads]
        q_i = [q_ref[0, pl.ds(q0, tq), c] for c in cols]
        acc_ref[...] = jnp.zeros(acc_ref.shape, jnp.float32)
        car_ref[...] = jnp.zeros(car_ref.shape, jnp.float32)

        def past_j0(n):
            return pl.multiple_of(jnp.maximum(2 * i - 1 - n, 0) * t, t)

        def stage_logits(z_ref, n):
            for hh, z in enumerate(logits(q_i, past_j0(n))):
                z_ref[hh] = z

        zs_low = logits([q_ref[0, pl.ds(q1, t), c] for c in cols], q1)
        zs_diag = logits(q_i, q0)
        stage_logits(za_ref, 0)
        tiles(zs_low, q1, slice(t, tq), True)
        tiles(zs_diag, q0, slice(0, tq), True)

        def past_pair(m, _):
            stage_logits(zb_ref, 2 * m + 1)
            tiles([za_ref[hh] for hh in heads], past_j0(2 * m), slice(0, tq), False)
            stage_logits(za_ref, 2 * m + 2)
            tiles([zb_ref[hh] for hh in heads], past_j0(2 * m + 1), slice(0, tq), False)
            return 0

        lax.fori_loop(0, i, past_pair, 0)
        acc = jnp.concatenate([acc_ref[hh] for hh in heads], axis=-1)
        o_ref[0, pl.ds(q0, tq), :] = (acc * _silu(g_ref[0, pl.ds(q0, tq), :])).astype(o_ref.dtype)
        return 0

    lax.fori_loop(0, nq, q_block, 0)


def _stick_breaking(qkv, g, heads):
    bsz, s, _ = qkv.shape
    assert s % (2 * SB_BLOCK) == 0 and heads % SB_HEADS_PER_STEP == 0
    steps = heads // SB_HEADS_PER_STEP
    width = SB_HEADS_PER_STEP * HEAD_DIM

    def col(group):
        return pl.BlockSpec((1, s, width), lambda h, b: (b, 0, h + group * steps))

    return pl.pallas_call(
        functools.partial(_sb_kernel, nq=s // (2 * SB_BLOCK)),
        out_shape=jax.ShapeDtypeStruct((bsz, s, heads * HEAD_DIM), jnp.bfloat16),
        grid=(steps, bsz),
        in_specs=[col(0), col(1), col(2), col(0)],
        out_specs=col(0),
        scratch_shapes=[pltpu.VMEM((SB_HEADS_PER_STEP, 2 * SB_BLOCK, SB_BLOCK), jnp.float32)] * 2
        + [pltpu.VMEM((SB_HEADS_PER_STEP, 2 * SB_BLOCK, HEAD_DIM), jnp.float32),
           pltpu.VMEM((SB_HEADS_PER_STEP, 2 * SB_BLOCK, 1), jnp.float32)],
        compiler_params=_cparams(2),
        name="stick_breaking",
    )(qkv, qkv, qkv, g)


def _xattn_kernel(q_ref, g_ref, k_ref, v_ref, o_ref, *, dh):
    scale = dh ** -0.5
    for h in range(XATTN_HEADS):
        cols = slice(h * dh, (h + 1) * dh)
        s = lax.dot_general(q_ref[0, :, cols], k_ref[0, :, cols], _NT,
                            preferred_element_type=jnp.float32) * scale
        m = jnp.max(s, axis=-1, keepdims=True)
        p = jnp.exp(s - m)
        l = jnp.sum(p, axis=-1, keepdims=True)
        o = jnp.dot(p.astype(jnp.bfloat16), v_ref[0, :, cols],
                    preferred_element_type=jnp.float32) / l
        o_ref[0, :, cols] = (o * _silu(g_ref[0, :, cols])).astype(o_ref.dtype)


def _xattn(q, g, kv, bm=512):
    bsz, s, w = q.shape
    n_mem = kv.shape[1]
    bm = min(bm, s)
    row = pl.BlockSpec((1, bm, w), lambda b, i: (b, i, 0))
    return pl.pallas_call(
        functools.partial(_xattn_kernel, dh=w // XATTN_HEADS),
        out_shape=jax.ShapeDtypeStruct((bsz, s, w), jnp.bfloat16),
        grid=(bsz, s // bm),
        in_specs=[row, row,
                  pl.BlockSpec((1, n_mem, w), lambda b, i: (b, 0, 0)),
                  pl.BlockSpec((1, n_mem, w), lambda b, i: (b, 0, 1))],
        out_specs=row,
        compiler_params=_cparams(2),
        name="xattn",
    )(q, g, kv, kv)


def _ln_proj_kernel(x_ref, y_ref, g_ref, b_ref, w_ref, xf_ref, q_ref, gate_ref, a0_ref, a1_ref, *,
                    q_blocks):
    i = pl.program_id(0)
    j = pl.program_id(1)
    chunk = x_ref.shape[0]
    a_refs = (a0_ref, a1_ref)

    def normalise(a_ref):
        o = _layer_norm_rows(DEEPNORM_ALPHA * x_ref[...] + y_ref[...], g_ref[...], b_ref[...])
        xf_ref[...] = o
        a_ref[pl.ds(pl.multiple_of(j * chunk, chunk), chunk), :] = o.astype(jnp.bfloat16)

    def emit(acc):
        @pl.when(j < q_blocks)
        def _():
            q_ref[...] = acc.astype(q_ref.dtype)

        @pl.when(j >= q_blocks)
        def _():
            gate_ref[...] = acc

    @pl.when(i == 0)
    def _():
        normalise(a_refs[0])

    for parity in (0, 1):
        @pl.when((i > 0) & (i % 2 == parity))
        def _():
            acc = jnp.dot(a_refs[1 - parity][...], w_ref[...], preferred_element_type=jnp.float32)
            normalise(a_refs[parity])
            emit(acc)


def _residual_ln_proj(x, y, g, b, w, q_cols, bm=1024, bn=256):
    m, d = x.shape
    n = w.shape[1]
    bm = min(bm, m)
    bn = math.gcd(bn, q_cols, n - q_cols)
    nj = n // bn
    chunk = bm // nj
    n_tiles = m // bm
    q_blocks = q_cols // bn
    assert m % bm == 0 and bm % nj == 0 and chunk % 16 == 0 and 0 < q_blocks < nj

    def chunk_idx(i, j):
        return (jnp.where(i < n_tiles, i * nj + j, n_tiles * nj - 1), 0)

    def out_idx(col):
        return lambda i, j: (jnp.maximum(i - 1, 0), jnp.where(i == 0, 0, col(j)))

    rows = pl.BlockSpec((chunk, d), chunk_idx)
    vec = pl.BlockSpec((1, d), lambda i, j: (0, 0))
    return pl.pallas_call(
        functools.partial(_ln_proj_kernel, q_blocks=q_blocks),
        out_shape=(jax.ShapeDtypeStruct((m, d), jnp.float32),
                   jax.ShapeDtypeStruct((m, q_cols), jnp.bfloat16),
                   jax.ShapeDtypeStruct((m, n - q_cols), jnp.float32)),
        grid=(n_tiles + 1, nj),
        in_specs=[rows, rows, vec, vec, pl.BlockSpec((d, bn), lambda i, j: (0, j))],
        out_specs=(rows,
                   pl.BlockSpec((bm, bn), out_idx(lambda j: jnp.minimum(j, q_blocks - 1))),
                   pl.BlockSpec((bm, bn), out_idx(lambda j: jnp.maximum(j - q_blocks, 0)))),
        scratch_shapes=[pltpu.VMEM((bm, d), jnp.bfloat16)] * 2,
        compiler_params=_cparams(2),
        name="residual_ln_proj",
    )(x, y, g.reshape(1, d), b.reshape(1, d), w)


def _memory_block(x_f32, y, g1, b1, mem_bf16, w_xq, w_xkv, w_xo, g2, b2, bsz, s, emit_bf16):
    xw = w_xo.shape[0]
    x_f32, q, g = _residual_ln_proj(x_f32, y, g1, b1, w_xq, xw)
    kv = _matmul(mem_bf16, w_xkv, 0, 2 * xw, jnp.bfloat16)
    n_mem = mem_bf16.shape[0] // bsz
    o = _xattn(q.reshape(bsz, s, xw), g.reshape(bsz, s, xw), kv.reshape(bsz, n_mem, 2 * xw))
    return _matmul_residual_ln(o.reshape(bsz * s, xw), w_xo, x_f32, g2, b2, emit_bf16)


def kernel(x, mem, rel_bias, w_in_0, conv_w_0, w_out_0, ln1_g_0, ln1_b_0, w_xq_0, w_xkv_0, w_xo_0, ln2_g_0, ln2_b_0, w_in_1, w_out_1, ln1_g_1, ln1_b_1, w_xq_1, w_xkv_1, w_xo_1, ln2_g_1, ln2_b_1):
    bsz, s, d = x.shape
    bf = jnp.bfloat16
    xf = x.reshape(bsz * s, d)
    xb = xf.astype(bf)
    memb = mem.reshape(-1, d).astype(bf)

    cw = conv_w_0.shape[1]
    mw = w_in_0.shape[1] // 4 - cw
    heads = mw // HEAD_DIM
    w_in = w_in_0.astype(bf)
    y_a = _conv_proj(xb, w_in, conv_w_0, cw, s)
    qkv = _matmul(xb, w_in, 4 * cw, 3 * mw, bf, scaled_cols=mw, scale=HEAD_DIM ** -0.5 * LOG2E)
    g_b = _matmul(xb, w_in, 4 * cw + 3 * mw, mw, jnp.float32)
    y_b = _moba(qkv.reshape(bsz, s, 3 * mw), g_b.reshape(bsz, s, mw), rel_bias.T, heads)
    y = _matmul_cat(y_a, y_b.reshape(bsz * s, mw), w_out_0.astype(bf), jnp.float32)
    xf, xb = _memory_block(xf, y, ln1_g_0, ln1_b_0, memb, w_xq_0.astype(bf), w_xkv_0.astype(bf),
                           w_xo_0.astype(bf), ln2_g_0, ln2_b_0, bsz, s, True)

    sw = w_out_1.shape[0]
    w_in = w_in_1.astype(bf)
    qkv = _matmul(xb, w_in, 0, 3 * sw, bf, scaled_cols=sw, scale=HEAD_DIM ** -0.5 * LOG2E)
    g_c = _matmul(xb, w_in, 3 * sw, sw, jnp.float32)
    y_c = _stick_breaking(qkv.reshape(bsz, s, 3 * sw), g_c.reshape(bsz, s, sw), sw // HEAD_DIM)
    y = _matmul(y_c.reshape(bsz * s, sw), w_out_1.astype(bf), 0, d, jnp.float32)
    xf, _ = _memory_block(xf, y, ln1_g_1, ln1_b_1, memb, w_xq_1.astype(bf), w_xkv_1.astype(bf),
                          w_xo_1.astype(bf), ln2_g_1, ln2_b_1, bsz, s, False)
    return xf.reshape(bsz, s, d)
```

```python
import functools
import math

import numpy as np
import jax
import jax.numpy as jnp
from jax import lax
from jax.experimental import pallas as pl
from jax.experimental.pallas import tpu as pltpu

DEPTH = 2
HEAD_DIM = 128
CONV_K = 3
MOBA_BLOCK = 256
MOBA_TOPK = 3
XATTN_HEADS = 4
REL_BUCKETS = 32
REL_MAX_DIST = 128
LN_EPS = 1e-5
NEG_INF = -1e30
DEEPNORM_ALPHA = (2 * DEPTH) ** 0.25
LOG2E = math.log2(math.e)

MOBA_HEADS_PER_STEP = 2
SB_BLOCK = 256
SB_HEADS_PER_STEP = 4
SB_MAX_LOG2_ODDS = 60.0
SB_DEAD_BITS = 160.0
VMEM_LIMIT = 56 * 1024 * 1024

_NT = (((1,), (1,)), ((), ()))


def _cparams(n_axes):
    return pltpu.CompilerParams(dimension_semantics=("arbitrary",) * n_axes,
                                vmem_limit_bytes=VMEM_LIMIT)


def _silu(g):
    return g * jax.nn.sigmoid(g)


def _split_bf16(x):
    hi = x.astype(jnp.bfloat16)
    lo = (x - hi.astype(jnp.float32)).astype(jnp.bfloat16)
    return hi, lo


def _mm_kernel(a_ref, w_ref, o_ref, *, scaled_blocks, scale):
    acc = jnp.dot(a_ref[...], w_ref[...], preferred_element_type=jnp.float32)
    if scaled_blocks:
        acc = acc * jnp.where(pl.program_id(1) < scaled_blocks, scale, 1.0)
    o_ref[...] = acc.astype(o_ref.dtype)


def _matmul(a, w, n_start, n_size, out_dtype, bm=1024, bn=1024, scaled_cols=0, scale=1.0):
    m, k = a.shape
    bm = min(bm, m)
    bn = math.gcd(math.gcd(math.gcd(bn, n_size), n_start), scaled_cols)
    assert m % bm == 0 and bn % 128 == 0
    joff = n_start // bn
    return pl.pallas_call(
        functools.partial(_mm_kernel, scaled_blocks=scaled_cols // bn, scale=scale),
        out_shape=jax.ShapeDtypeStruct((m, n_size), out_dtype),
        grid=(m // bm, n_size // bn),
        in_specs=[pl.BlockSpec((bm, k), lambda i, j: (i, 0)),
                  pl.BlockSpec((k, bn), lambda i, j: (0, j + joff))],
        out_specs=pl.BlockSpec((bm, bn), lambda i, j: (i, j)),
        compiler_params=_cparams(2),
        name="matmul",
    )(a, w)


def _mm2_kernel(a1_ref, a2_ref, w_ref, o_ref):
    k1 = a1_ref.shape[1]
    acc = jnp.dot(a1_ref[...], w_ref[0:k1, :], preferred_element_type=jnp.float32)
    acc = acc + jnp.dot(a2_ref[...], w_ref[k1:, :], preferred_element_type=jnp.float32)
    o_ref[...] = acc.astype(o_ref.dtype)


def _matmul_cat(a1, a2, w, out_dtype, bm=1024, bn=1024):
    m, k1 = a1.shape
    k2 = a2.shape[1]
    n = w.shape[1]
    bm = min(bm, m)
    bn = min(bn, n)
    assert m % bm == 0 and n % bn == 0 and w.shape[0] == k1 + k2
    return pl.pallas_call(
        _mm2_kernel,
        out_shape=jax.ShapeDtypeStruct((m, n), out_dtype),
        grid=(m // bm, n // bn),
        in_specs=[pl.BlockSpec((bm, k1), lambda i, j: (i, 0)),
                  pl.BlockSpec((bm, k2), lambda i, j: (i, 0)),
                  pl.BlockSpec((k1 + k2, bn), lambda i, j: (0, j))],
        out_specs=pl.BlockSpec((bm, bn), lambda i, j: (i, j)),
        compiler_params=_cparams(2),
        name="matmul_cat",
    )(a1, a2, w)


def _layer_norm_rows(r, g, b):
    mu = jnp.mean(r, axis=-1, keepdims=True)
    d = r - mu
    var = jnp.mean(d * d, axis=-1, keepdims=True)
    return d * lax.rsqrt(var + LN_EPS) * g + b


def _mm_ln_pipelined_kernel(a_ref, w_ref, x_ref, g_ref, b_ref, of_ref, *rest):
    ob_ref = rest[0] if len(rest) == 3 else None
    y_refs = rest[-2:]
    i = pl.program_id(0)

    def multiply(y_ref):
        y_ref[...] = jnp.dot(a_ref[...], w_ref[...], preferred_element_type=jnp.float32)

    def normalise(y_ref):
        o = _layer_norm_rows(DEEPNORM_ALPHA * x_ref[...] + y_ref[...], g_ref[...], b_ref[...])
        of_ref[...] = o
        if ob_ref is not None:
            ob_ref[...] = o.astype(jnp.bfloat16)

    @pl.when(i == 0)
    def _():
        multiply(y_refs[0])

    for parity in (0, 1):
        @pl.when((i > 0) & (i % 2 == parity))
        def _():
            multiply(y_refs[parity])
            normalise(y_refs[1 - parity])


def _matmul_residual_ln(a, w, x, g, b, emit_bf16, bm=256):
    m, k = a.shape
    d = w.shape[1]
    bm = min(bm, m)
    n_tiles = m // bm
    lagged = pl.BlockSpec((bm, d), lambda i: (jnp.maximum(i - 1, 0), 0))
    vec = pl.BlockSpec((1, d), lambda i: (0, 0))
    out_shape = [jax.ShapeDtypeStruct((m, d), jnp.float32)]
    if emit_bf16:
        out_shape.append(jax.ShapeDtypeStruct((m, d), jnp.bfloat16))
    out = pl.pallas_call(
        _mm_ln_pipelined_kernel,
        out_shape=tuple(out_shape),
        grid=(n_tiles + 1,),
        in_specs=[pl.BlockSpec((bm, k), lambda i: (jnp.minimum(i, n_tiles - 1), 0)),
                  pl.BlockSpec((k, d), lambda i: (0, 0)), lagged, vec, vec],
        out_specs=tuple([lagged] * len(out_shape)),
        scratch_shapes=[pltpu.VMEM((bm, d), jnp.float32)] * 2,
        compiler_params=_cparams(1),
        name="matmul_residual_ln",
    )(a, w, x, g.reshape(1, d), b.reshape(1, d))
    return out if emit_bf16 else (out[0], None)


def _conv_proj_kernel(a_ref, wh_ref, wb_ref, wc_ref, wg_ref, cw_ref, o_ref, tail_ref, *,
                      tiles_per_seq):
    i = pl.program_id(0)
    j = pl.program_id(1)
    a = a_ref[...]
    u = (jnp.dot(a, wc_ref[...], preferred_element_type=jnp.float32)
         * jnp.dot(a, wh_ref[...], preferred_element_type=jnp.float32))
    bm = u.shape[0]

    @pl.when(i % tiles_per_seq == 0)
    def _():
        tail_ref[j] = jnp.zeros(tail_ref.shape[1:], jnp.float32)

    prev = tail_ref[j]
    tail_ref[j] = u[bm - 8:bm, :]
    row = lax.broadcasted_iota(jnp.int32, u.shape, 0)
    u1 = jnp.where(row == 0, prev[7:8, :], pltpu.roll(u, 1, 0))
    u2 = jnp.where(row == 0, prev[6:7, :],
                   jnp.where(row == 1, prev[7:8, :], pltpu.roll(u, 2, 0)))
    w = cw_ref[...]
    conv = w[0:1, :] * u2 + w[1:2, :] * u1 + w[2:3, :] * u
    b = jnp.dot(a, wb_ref[...], preferred_element_type=jnp.float32)
    g = jnp.dot(a, wg_ref[...], preferred_element_type=jnp.float32)
    o_ref[...] = (b * conv * _silu(g)).astype(o_ref.dtype)


def _conv_proj(a, w_in, conv_w, width, seq_len, bm=1024, bc=256):
    m, k = a.shape
    bm = min(bm, seq_len)
    bc = min(bc, width)
    nc = width // bc
    assert seq_len % bm == 0 and m % seq_len == 0 and width % bc == 0 and bm % 8 == 0

    def wcol(group):
        return pl.BlockSpec((k, bc), lambda i, j: (0, j + group * nc))

    return pl.pallas_call(
        functools.partial(_conv_proj_kernel, tiles_per_seq=seq_len // bm),
        out_shape=jax.ShapeDtypeStruct((m, width), jnp.bfloat16),
        grid=(m // bm, nc),
        in_specs=[pl.BlockSpec((bm, k), lambda i, j: (i, 0)),
                  wcol(0), wcol(1), wcol(2), wcol(3),
                  pl.BlockSpec((CONV_K, bc), lambda i, j: (0, j))],
        out_specs=pl.BlockSpec((bm, bc), lambda i, j: (i, j)),
        scratch_shapes=[pltpu.VMEM((nc, 8, bc), jnp.float32)],
        compiler_params=_cparams(2),
        name="conv_proj",
    )(a, w_in, w_in, w_in, w_in, conv_w)


def _t5_bucket_thresholds():
    d = np.arange(0, 4 * REL_MAX_DIST)
    max_exact = REL_BUCKETS // 2
    nf = np.maximum(d, 1).astype(np.float32)
    large = max_exact + (np.log(nf / np.float32(max_exact)) / np.float32(math.log(REL_MAX_DIST / max_exact))
                         * np.float32(REL_BUCKETS - max_exact)).astype(np.int32)
    large = np.minimum(large, REL_BUCKETS - 1)
    bucket = np.where(d < max_exact, d, large)
    assert np.all(np.diff(bucket) >= 0) and bucket[-1] == REL_BUCKETS - 1
    return [int(np.argmax(bucket >= b)) for b in range(REL_BUCKETS)]


_BUCKET_THR = _t5_bucket_thresholds()
_MOBA_BLOCK_SHIFT = MOBA_BLOCK.bit_length() - 1
assert 1 << _MOBA_BLOCK_SHIFT == MOBA_BLOCK


def _moba_kernel(rb_ref, q_ref, k_ref, v_ref, g_ref, o_ref, t0_ref, t1_ref, kaug_ref, *, nb):
    blk = MOBA_BLOCK
    s_len = nb * blk
    heads = range(MOBA_HEADS_PER_STEP)
    head_ids = [pl.program_id(0) * MOBA_HEADS_PER_STEP + hh for hh in heads]
    cols = [slice(hh * HEAD_DIM, (hh + 1) * HEAD_DIM) for hh in heads]
    qo = lax.broadcasted_iota(jnp.int32, (blk, blk), 0)
    ko = lax.broadcasted_iota(jnp.int32, (blk, blk), 1)

    @pl.when(pl.program_id(1) == 0)
    def _():
        def table(h, dist):
            t = jnp.full(dist.shape, rb_ref[h, REL_BUCKETS - 1], jnp.float32)
            for bkt in range(REL_BUCKETS - 2, -1, -1):
                t = jnp.where(dist < _BUCKET_THR[bkt + 1], rb_ref[h, bkt], t)
            return t * LOG2E
        for hh in heads:
            t0_ref[hh] = table(head_ids[hh], qo - ko)
            t1_ref[hh] = table(head_ids[hh], qo - ko + blk)

    rb_far = [rb_ref[h, REL_BUCKETS - 1] * LOG2E for h in head_ids]
    assert _BUCKET_THR[REL_BUCKETS - 1] <= blk + 1

    block_onehot = jnp.where(
        jnp.right_shift(lax.broadcasted_iota(jnp.int32, (s_len, HEAD_DIM), 0), _MOBA_BLOCK_SHIFT)
        == lax.broadcasted_iota(jnp.int32, (s_len, HEAD_DIM), 1), 1.0, 0.0).astype(jnp.bfloat16)
    kbars = []
    for hh in heads:
        kaug_ref[hh, :, 0:HEAD_DIM] = k_ref[0, :, cols[hh]]
        kaug_ref[hh, :, HEAD_DIM:2 * HEAD_DIM] = block_onehot
        kbar = jnp.concatenate(
            [jnp.mean(k_ref[0, j * blk:(j + 1) * blk, cols[hh]].astype(jnp.float32),
                      axis=0, keepdims=True) for j in range(nb)]
            + [jnp.zeros((16 - nb, HEAD_DIM), jnp.float32)], axis=0)
        kbars.append(_split_bf16(kbar))

    eye = jnp.where(qo == ko, 1.0, 0.0).astype(jnp.bfloat16)
    jidx = lax.broadcasted_iota(jnp.int32, (16, blk), 0)

    def logits(hh, i):
        q_i = q_ref[0, i * blk:(i + 1) * blk, cols[hh]]
        kbar_hi, kbar_lo = kbars[hh]
        nk = (i + 1) * blk
        if i > MOBA_TOPK:
            gate = (lax.dot_general(kbar_hi, q_i, _NT, preferred_element_type=jnp.float32)
                    + lax.dot_general(kbar_lo, q_i, _NT, preferred_element_type=jnp.float32))
            cnt = jnp.zeros((16, blk), jnp.float32)
            for jp in range(i):
                row = gate[jp:jp + 1, :]
                beats = jnp.where(row > gate, 1.0,
                                  jnp.where(row == gate, jnp.where(jidx > jp, 1.0, 0.0), 0.0))
                cnt = cnt + beats
            drop_t = jnp.where(cnt < MOBA_TOPK, 0.0, jnp.where(jidx < i, NEG_INF, 0.0))
            drop_t = jnp.concatenate([drop_t, jnp.zeros((128 - 16, blk), jnp.float32)], axis=0)
            drop = lax.dot_general(eye, drop_t.astype(jnp.bfloat16), _NT,
                                   preferred_element_type=jnp.float32)
            q_aug = jnp.concatenate([q_i, drop.astype(jnp.bfloat16)], axis=-1)
            return lax.dot_general(q_aug, kaug_ref[hh, 0:nk, :], _NT,
                                   preferred_element_type=jnp.float32)
        return lax.dot_general(q_i, k_ref[0, 0:nk, cols[hh]], _NT,
                               preferred_element_type=jnp.float32)

    def attend(hh, i, s):
        nk = (i + 1) * blk
        n_far = max(i - 1, 0) * blk
        lg_own = jnp.where(ko <= qo, s[:, i * blk:nk] + t0_ref[hh], NEG_INF)
        m = jnp.max(lg_own, axis=-1, keepdims=True)
        if i > 0:
            lg_prev = s[:, (i - 1) * blk:i * blk] + t1_ref[hh]
            m = jnp.maximum(m, jnp.max(lg_prev, axis=-1, keepdims=True))
        if n_far:
            m = jnp.maximum(m, jnp.max(s[:, 0:n_far], axis=-1, keepdims=True) + rb_far[hh])
        pieces = [jnp.exp2(lg_own - m)]
        if i > 0:
            pieces.insert(0, jnp.exp2(lg_prev - m))
        if n_far:
            pieces.insert(0, jnp.exp2(s[:, 0:n_far] - (m - rb_far[hh])))
        l = sum(jnp.sum(piece, axis=-1, keepdims=True) for piece in pieces)
        p = pieces[0] if len(pieces) == 1 else jnp.concatenate(pieces, axis=-1)
        o = jnp.dot(p.astype(jnp.bfloat16), v_ref[0, 0:nk, cols[hh]],
                    preferred_element_type=jnp.float32) / l
        rows = slice(i * blk, (i + 1) * blk)
        o_ref[0, rows, cols[hh]] = (o * _silu(g_ref[0, rows, cols[hh]])).astype(o_ref.dtype)

    items = [(hh, i) for hh in heads for i in range(nb)]
    s_next = logits(*items[0])
    for n, item in enumerate(items):
        s_cur, s_next = s_next, (logits(*items[n + 1]) if n + 1 < len(items) else None)
        attend(*item, s_cur)


def _moba(qkv, g, rel_bias_t, heads):
    bsz, s, _ = qkv.shape
    assert s % MOBA_BLOCK == 0
    nb = s // MOBA_BLOCK
    assert MOBA_TOPK < nb <= 16 and heads % MOBA_HEADS_PER_STEP == 0
    steps = heads // MOBA_HEADS_PER_STEP
    width = MOBA_HEADS_PER_STEP * HEAD_DIM

    def col(group):
        return pl.BlockSpec((1, s, width), lambda h, b: (b, 0, h + group * steps))

    return pl.pallas_call(
        functools.partial(_moba_kernel, nb=nb),
        out_shape=jax.ShapeDtypeStruct((bsz, s, heads * HEAD_DIM), jnp.bfloat16),
        grid=(steps, bsz),
        in_specs=[pl.BlockSpec(memory_space=pltpu.SMEM), col(0), col(1), col(2), col(0)],
        out_specs=col(0),
        scratch_shapes=[pltpu.VMEM((MOBA_HEADS_PER_STEP, MOBA_BLOCK, MOBA_BLOCK), jnp.float32),
                        pltpu.VMEM((MOBA_HEADS_PER_STEP, MOBA_BLOCK, MOBA_BLOCK), jnp.float32),
                        pltpu.VMEM((MOBA_HEADS_PER_STEP, s, 2 * HEAD_DIM), jnp.bfloat16)],
        compiler_params=_cparams(2),
        name="moba",
    )(rel_bias_t, qkv, qkv, qkv, g)


def _sb_kernel(q_ref, k_ref, v_ref, g_ref, o_ref, za_ref, zb_ref, acc_ref, car_ref, *, nq):
    t = SB_BLOCK
    tq = 2 * t
    tri = jnp.where(lax.broadcasted_iota(jnp.int32, (t, t), 0)
                    > lax.broadcasted_iota(jnp.int32, (t, t), 1), 1.0, 0.0).astype(jnp.bfloat16)

    def logits(qs, j0):
        return tuple(lax.dot_general(q, k_ref[0, pl.ds(j0, t), hh * HEAD_DIM:(hh + 1) * HEAD_DIM],
                                     _NT, preferred_element_type=jnp.float32)
                     for hh, q in enumerate(qs))

    def tiles(zs, j0, rows, diag):
        heads = range(len(zs))
        cols = [slice(hh * HEAD_DIM, (hh + 1) * HEAD_DIM) for hh in heads]
        if diag:
            strict = (lax.broadcasted_iota(jnp.int32, zs[0].shape, 1)
                      < lax.broadcasted_iota(jnp.int32, zs[0].shape, 0))
            zs = [jnp.where(strict, z, NEG_INF) for z in zs]
        zs = [jnp.minimum(z, SB_MAX_LOG2_ODDS) for z in zs]
        sps = [jnp.log(1.0 + jnp.exp2(z)) * LOG2E for z in zs]
        rems = [jnp.dot(sps[hh].astype(jnp.bfloat16), tri, preferred_element_type=jnp.float32)
                + car_ref[hh, rows, :] for hh in heads]
        ws = [jnp.exp2((zs[hh] - sps[hh]) - rems[hh]).astype(jnp.bfloat16) for hh in heads]
        for hh in heads:
            acc_ref[hh, rows, :] += jnp.dot(ws[hh], v_ref[0, pl.ds(j0, t), cols[hh]],
                                            preferred_element_type=jnp.float32)
            car_ref[hh, rows, :] = rems[hh][:, 0:1] + sps[hh][:, 0:1]

    def q_block(i, _):
        q0 = pl.multiple_of(i * tq, tq)
        q1 = pl.multiple_of(i * tq + t, t)
        heads = range(SB_HEADS_PER_STEP)
        cols = [slice(hh * HEAD_DIM, (hh + 1) * HEAD_DIM) for hh in heads]
        q_i = [q_ref[0, pl.ds(q0, tq), c] for c in cols]
        acc_ref[...] = jnp.zeros(acc_ref.shape, jnp.float32)
        car_ref[...] = jnp.zeros(car_ref.shape, jnp.float32)

        def past_j0(n):
            return pl.multiple_of(jnp.maximum(2 * i - 1 - n, 0) * t, t)

        def stage_logits(z_ref, n):
            for hh, z in enumerate(logits(q_i, past_j0(n))):
                z_ref[hh] = z

        zs_low = logits([q_ref[0, pl.ds(q1, t), c] for c in cols], q1)
        zs_diag = logits(q_i, q0)
        stage_logits(za_ref, 0)
        tiles(zs_low, q1, slice(t, tq), True)
        tiles(zs_diag, q0, slice(0, tq), True)

        def all_dead():
            return (jnp.min(jnp.min(car_ref[...], axis=0)) >= SB_DEAD_BITS).astype(jnp.int32)

        def past_pair(carry):
            m = carry[0]
            stage_logits(zb_ref, 2 * m + 1)
            tiles([za_ref[hh] for hh in heads], past_j0(2 * m), slice(0, tq), False)

            @pl.when(all_dead() == 0)
            def _():
                stage_logits(za_ref, 2 * m + 2)
                tiles([zb_ref[hh] for hh in heads], past_j0(2 * m + 1), slice(0, tq), False)

            return m + 1, all_dead()

        lax.while_loop(lambda carry: (carry[0] < i) & (carry[1] == 0), past_pair,
                       (jnp.int32(0), jnp.int32(0)))
        acc = jnp.concatenate([acc_ref[hh] for hh in heads], axis=-1)
        o_ref[0, pl.ds(q0, tq), :] = (acc * _silu(g_ref[0, pl.ds(q0, tq), :])).astype(o_ref.dtype)
        return 0

    lax.fori_loop(0, nq, q_block, 0)


def _stick_breaking(qkv, g, heads):
    bsz, s, _ = qkv.shape
    assert s % (2 * SB_BLOCK) == 0 and heads % SB_HEADS_PER_STEP == 0
    steps = heads // SB_HEADS_PER_STEP
    width = SB_HEADS_PER_STEP * HEAD_DIM

    def col(group):
        return pl.BlockSpec((1, s, width), lambda h, b: (b, 0, h + group * steps))

    return pl.pallas_call(
        functools.partial(_sb_kernel, nq=s // (2 * SB_BLOCK)),
        out_shape=jax.ShapeDtypeStruct((bsz, s, heads * HEAD_DIM), jnp.bfloat16),
        grid=(steps, bsz),
        in_specs=[col(0), col(1), col(2), col(0)],
        out_specs=col(0),
        scratch_shapes=[pltpu.VMEM((SB_HEADS_PER_STEP, 2 * SB_BLOCK, SB_BLOCK), jnp.float32)] * 2
        + [pltpu.VMEM((SB_HEADS_PER_STEP, 2 * SB_BLOCK, HEAD_DIM), jnp.float32),
           pltpu.VMEM((SB_HEADS_PER_STEP, 2 * SB_BLOCK, 1), jnp.float32)],
        compiler_params=_cparams(2),
        name="stick_breaking",
    )(qkv, qkv, qkv, g)


def _xattn_kernel(q_ref, g_ref, k_ref, v_ref, o_ref, *, dh):
    scale = dh ** -0.5
    for h in range(XATTN_HEADS):
        cols = slice(h * dh, (h + 1) * dh)
        s = lax.dot_general(q_ref[0, :, cols], k_ref[0, :, cols], _NT,
                            preferred_element_type=jnp.float32) * scale
        m = jnp.max(s, axis=-1, keepdims=True)
        p = jnp.exp(s - m)
        l = jnp.sum(p, axis=-1, keepdims=True)
        o = jnp.dot(p.astype(jnp.bfloat16), v_ref[0, :, cols],
                    preferred_element_type=jnp.float32) / l
        o_ref[0, :, cols] = (o * _silu(g_ref[0, :, cols])).astype(o_ref.dtype)


def _xattn(q, g, kv, bm=512):
    bsz, s, w = q.shape
    n_mem = kv.shape[1]
    bm = min(bm, s)
    row = pl.BlockSpec((1, bm, w), lambda b, i: (b, i, 0))
    return pl.pallas_call(
        functools.partial(_xattn_kernel, dh=w // XATTN_HEADS),
        out_shape=jax.ShapeDtypeStruct((bsz, s, w), jnp.bfloat16),
        grid=(bsz, s // bm),
        in_specs=[row, row,
                  pl.BlockSpec((1, n_mem, w), lambda b, i: (b, 0, 0)),
                  pl.BlockSpec((1, n_mem, w), lambda b, i: (b, 0, 1))],
        out_specs=row,
        compiler_params=_cparams(2),
        name="xattn",
    )(q, g, kv, kv)


def _ln_proj_kernel(x_ref, y_ref, g_ref, b_ref, w_ref, xf_ref, q_ref, gate_ref, a0_ref, a1_ref, *,
                    q_blocks):
    i = pl.program_id(0)
    j = pl.program_id(1)
    chunk = x_ref.shape[0]
    a_refs = (a0_ref, a1_ref)

    def normalise(a_ref):
        o = _layer_norm_rows(DEEPNORM_ALPHA * x_ref[...] + y_ref[...], g_ref[...], b_ref[...])
        xf_ref[...] = o
        a_ref[pl.ds(pl.multiple_of(j * chunk, chunk), chunk), :] = o.astype(jnp.bfloat16)

    def emit(acc):
        @pl.when(j < q_blocks)
        def _():
            q_ref[...] = acc.astype(q_ref.dtype)

        @pl.when(j >= q_blocks)
        def _():
            gate_ref[...] = acc

    @pl.when(i == 0)
    def _():
        normalise(a_refs[0])

    for parity in (0, 1):
        @pl.when((i > 0) & (i % 2 == parity))
        def _():
            acc = jnp.dot(a_refs[1 - parity][...], w_ref[...], preferred_element_type=jnp.float32)
            normalise(a_refs[parity])
            emit(acc)


def _residual_ln_proj(x, y, g, b, w, q_cols, bm=1024, bn=256):
    m, d = x.shape
    n = w.shape[1]
    bm = min(bm, m)
    bn = math.gcd(bn, q_cols, n - q_cols)
    nj = n // bn
    chunk = bm // nj
    n_tiles = m // bm
    q_blocks = q_cols // bn
    assert m % bm == 0 and bm % nj == 0 and chunk % 16 == 0 and 0 < q_blocks < nj

    def chunk_idx(i, j):
        return (jnp.where(i < n_tiles, i * nj + j, n_tiles * nj - 1), 0)

    def out_idx(col):
        return lambda i, j: (jnp.maximum(i - 1, 0), jnp.where(i == 0, 0, col(j)))

    rows = pl.BlockSpec((chunk, d), chunk_idx)
    vec = pl.BlockSpec((1, d), lambda i, j: (0, 0))
    return pl.pallas_call(
        functools.partial(_ln_proj_kernel, q_blocks=q_blocks),
        out_shape=(jax.ShapeDtypeStruct((m, d), jnp.float32),
                   jax.ShapeDtypeStruct((m, q_cols), jnp.bfloat16),
                   jax.ShapeDtypeStruct((m, n - q_cols), jnp.float32)),
        grid=(n_tiles + 1, nj),
        in_specs=[rows, rows, vec, vec, pl.BlockSpec((d, bn), lambda i, j: (0, j))],
        out_specs=(rows,
                   pl.BlockSpec((bm, bn), out_idx(lambda j: jnp.minimum(j, q_blocks - 1))),
                   pl.BlockSpec((bm, bn), out_idx(lambda j: jnp.maximum(j - q_blocks, 0)))),
        scratch_shapes=[pltpu.VMEM((bm, d), jnp.bfloat16)] * 2,
        compiler_params=_cparams(2),
        name="residual_ln_proj",
    )(x, y, g.reshape(1, d), b.reshape(1, d), w)


def _memory_block(x_f32, y, g1, b1, mem_bf16, w_xq, w_xkv, w_xo, g2, b2, bsz, s, emit_bf16):
    xw = w_xo.shape[0]
    x_f32, q, g = _residual_ln_proj(x_f32, y, g1, b1, w_xq, xw)
    kv = _matmul(mem_bf16, w_xkv, 0, 2 * xw, jnp.bfloat16)
    n_mem = mem_bf16.shape[0] // bsz
    o = _xattn(q.reshape(bsz, s, xw), g.reshape(bsz, s, xw), kv.reshape(bsz, n_mem, 2 * xw))
    return _matmul_residual_ln(o.reshape(bsz * s, xw), w_xo, x_f32, g2, b2, emit_bf16)


def kernel(x, mem, rel_bias, w_in_0, conv_w_0, w_out_0, ln1_g_0, ln1_b_0, w_xq_0, w_xkv_0, w_xo_0, ln2_g_0, ln2_b_0, w_in_1, w_out_1, ln1_g_1, ln1_b_1, w_xq_1, w_xkv_1, w_xo_1, ln2_g_1, ln2_b_1):
    bsz, s, d = x.shape
    bf = jnp.bfloat16
    xf = x.reshape(bsz * s, d)
    xb = xf.astype(bf)
    memb = mem.reshape(-1, d).astype(bf)

    cw = conv_w_0.shape[1]
    mw = w_in_0.shape[1] // 4 - cw
    heads = mw // HEAD_DIM
    w_in = w_in_0.astype(bf)
    y_a = _conv_proj(xb, w_in, conv_w_0, cw, s)
    qkv = _matmul(xb, w_in, 4 * cw, 3 * mw, bf, scaled_cols=mw, scale=HEAD_DIM ** -0.5 * LOG2E)
    g_b = _matmul(xb, w_in, 4 * cw + 3 * mw, mw, jnp.float32)
    y_b = _moba(qkv.reshape(bsz, s, 3 * mw), g_b.reshape(bsz, s, mw), rel_bias.T, heads)
    y = _matmul_cat(y_a, y_b.reshape(bsz * s, mw), w_out_0.astype(bf), jnp.float32)
    xf, xb = _memory_block(xf, y, ln1_g_0, ln1_b_0, memb, w_xq_0.astype(bf), w_xkv_0.astype(bf),
                           w_xo_0.astype(bf), ln2_g_0, ln2_b_0, bsz, s, True)

    sw = w_out_1.shape[0]
    w_in = w_in_1.astype(bf)
    qkv = _matmul(xb, w_in, 0, 3 * sw, bf, scaled_cols=sw, scale=HEAD_DIM ** -0.5 * LOG2E)
    g_c = _matmul(xb, w_in, 3 * sw, sw, jnp.float32)
    y_c = _stick_breaking(qkv.reshape(bsz, s, 3 * sw), g_c.reshape(bsz, s, sw), sw // HEAD_DIM)
    y = _matmul(y_c.reshape(bsz * s, sw), w_out_1.astype(bf), 0, d, jnp.float32)
    xf, _ = _memory_block(xf, y, ln1_g_1, ln1_b_1, memb, w_xq_1.astype(bf), w_xkv_1.astype(bf),
                          w_xo_1.astype(bf), ln2_g_1, ln2_b_1, bsz, s, False)
    return xf.reshape(bsz, s, d)
```

```python
import functools
import math

import numpy as np
import jax
import jax.numpy as jnp
from jax import lax
from jax.experimental import pallas as pl
from jax.experimental.pallas import tpu as pltpu

DEPTH = 2
HEAD_DIM = 128
CONV_K = 3
MOBA_BLOCK = 256
MOBA_TOPK = 3
XATTN_HEADS = 4
REL_BUCKETS = 32
REL_MAX_DIST = 128
LN_EPS = 1e-5
NEG_INF = -1e30
DEEPNORM_ALPHA = (2 * DEPTH) ** 0.25
LOG2E = math.log2(math.e)

MOBA_HEADS_PER_STEP = 2
SB_BLOCK = 256
SB_HEADS_PER_STEP = 4
SB_MAX_LOG2_ODDS = 60.0
SB_DEAD_BITS = 160.0
VMEM_LIMIT = 56 * 1024 * 1024

_NT = (((1,), (1,)), ((), ()))


def _cparams(n_axes):
    return pltpu.CompilerParams(dimension_semantics=("arbitrary",) * n_axes,
                                vmem_limit_bytes=VMEM_LIMIT)


def _silu(g):
    return g * jax.nn.sigmoid(g)


def _split_bf16(x):
    hi = x.astype(jnp.bfloat16)
    lo = (x - hi.astype(jnp.float32)).astype(jnp.bfloat16)
    return hi, lo


def _mm_kernel(a_ref, w_ref, o_ref, *, scaled_blocks, scale):
    acc = jnp.dot(a_ref[...], w_ref[...], preferred_element_type=jnp.float32)
    if scaled_blocks:
        acc = acc * jnp.where(pl.program_id(1) < scaled_blocks, scale, 1.0)
    o_ref[...] = acc.astype(o_ref.dtype)


def _matmul(a, w, n_start, n_size, out_dtype, bm=1024, bn=1024, scaled_cols=0, scale=1.0):
    m, k = a.shape
    bm = min(bm, m)
    bn = math.gcd(math.gcd(math.gcd(bn, n_size), n_start), scaled_cols)
    assert m % bm == 0 and bn % 128 == 0
    joff = n_start // bn
    return pl.pallas_call(
        functools.partial(_mm_kernel, scaled_blocks=scaled_cols // bn, scale=scale),
        out_shape=jax.ShapeDtypeStruct((m, n_size), out_dtype),
        grid=(m // bm, n_size // bn),
        in_specs=[pl.BlockSpec((bm, k), lambda i, j: (i, 0)),
                  pl.BlockSpec((k, bn), lambda i, j: (0, j + joff))],
        out_specs=pl.BlockSpec((bm, bn), lambda i, j: (i, j)),
        compiler_params=_cparams(2),
        name="matmul",
    )(a, w)


def _mm2_kernel(a1_ref, a2_ref, w_ref, o_ref):
    k1 = a1_ref.shape[1]
    acc = jnp.dot(a1_ref[...], w_ref[0:k1, :], preferred_element_type=jnp.float32)
    acc = acc + jnp.dot(a2_ref[...], w_ref[k1:, :], preferred_element_type=jnp.float32)
    o_ref[...] = acc.astype(o_ref.dtype)


def _matmul_cat(a1, a2, w, out_dtype, bm=1024, bn=1024):
    m, k1 = a1.shape
    k2 = a2.shape[1]
    n = w.shape[1]
    bm = min(bm, m)
    bn = min(bn, n)
    assert m % bm == 0 and n % bn == 0 and w.shape[0] == k1 + k2
    return pl.pallas_call(
        _mm2_kernel,
        out_shape=jax.ShapeDtypeStruct((m, n), out_dtype),
        grid=(m // bm, n // bn),
        in_specs=[pl.BlockSpec((bm, k1), lambda i, j: (i, 0)),
                  pl.BlockSpec((bm, k2), lambda i, j: (i, 0)),
                  pl.BlockSpec((k1 + k2, bn), lambda i, j: (0, j))],
        out_specs=pl.BlockSpec((bm, bn), lambda i, j: (i, j)),
        compiler_params=_cparams(2),
        name="matmul_cat",
    )(a1, a2, w)


def _layer_norm_rows(r, g, b):
    mu = jnp.mean(r, axis=-1, keepdims=True)
    d = r - mu
    var = jnp.mean(d * d, axis=-1, keepdims=True)
    return d * lax.rsqrt(var + LN_EPS) * g + b


def _mm_ln_pipelined_kernel(a_ref, w_ref, x_ref, g_ref, b_ref, of_ref, *rest):
    ob_ref = rest[0] if len(rest) == 3 else None
    y_refs = rest[-2:]
    i = pl.program_id(0)

    def multiply(y_ref):
        y_ref[...] = jnp.dot(a_ref[...], w_ref[...], preferred_element_type=jnp.float32)

    def normalise(y_ref):
        o = _layer_norm_rows(DEEPNORM_ALPHA * x_ref[...] + y_ref[...], g_ref[...], b_ref[...])
        of_ref[...] = o
        if ob_ref is not None:
            ob_ref[...] = o.astype(jnp.bfloat16)

    @pl.when(i == 0)
    def _():
        multiply(y_refs[0])

    for parity in (0, 1):
        @pl.when((i > 0) & (i % 2 == parity))
        def _():
            multiply(y_refs[parity])
            normalise(y_refs[1 - parity])


def _matmul_residual_ln(a, w, x, g, b, emit_bf16, bm=256):
    m, k = a.shape
    d = w.shape[1]
    bm = min(bm, m)
    n_tiles = m // bm
    lagged = pl.BlockSpec((bm, d), lambda i: (jnp.maximum(i - 1, 0), 0))
    vec = pl.BlockSpec((1, d), lambda i: (0, 0))
    out_shape = [jax.ShapeDtypeStruct((m, d), jnp.float32)]
    if emit_bf16:
        out_shape.append(jax.ShapeDtypeStruct((m, d), jnp.bfloat16))
    out = pl.pallas_call(
        _mm_ln_pipelined_kernel,
        out_shape=tuple(out_shape),
        grid=(n_tiles + 1,),
        in_specs=[pl.BlockSpec((bm, k), lambda i: (jnp.minimum(i, n_tiles - 1), 0)),
                  pl.BlockSpec((k, d), lambda i: (0, 0)), lagged, vec, vec],
        out_specs=tuple([lagged] * len(out_shape)),
        scratch_shapes=[pltpu.VMEM((bm, d), jnp.float32)] * 2,
        compiler_params=_cparams(1),
        name="matmul_residual_ln",
    )(a, w, x, g.reshape(1, d), b.reshape(1, d))
    return out if emit_bf16 else (out[0], None)


def _conv_proj_kernel(a_ref, wh_ref, wb_ref, wc_ref, wg_ref, cw_ref, o_ref, tail_ref, *,
                      tiles_per_seq):
    i = pl.program_id(0)
    j = pl.program_id(1)
    a = a_ref[...]
    u = (jnp.dot(a, wc_ref[...], preferred_element_type=jnp.float32)
         * jnp.dot(a, wh_ref[...], preferred_element_type=jnp.float32))
    bm = u.shape[0]

    @pl.when(i % tiles_per_seq == 0)
    def _():
        tail_ref[j] = jnp.zeros(tail_ref.shape[1:], jnp.float32)

    prev = tail_ref[j]
    tail_ref[j] = u[bm - 8:bm, :]
    row = lax.broadcasted_iota(jnp.int32, u.shape, 0)
    u1 = jnp.where(row == 0, prev[7:8, :], pltpu.roll(u, 1, 0))
    u2 = jnp.where(row == 0, prev[6:7, :],
                   jnp.where(row == 1, prev[7:8, :], pltpu.roll(u, 2, 0)))
    w = cw_ref[...]
    conv = w[0:1, :] * u2 + w[1:2, :] * u1 + w[2:3, :] * u
    b = jnp.dot(a, wb_ref[...], preferred_element_type=jnp.float32)
    g = jnp.dot(a, wg_ref[...], preferred_element_type=jnp.float32)
    o_ref[...] = (b * conv * _silu(g)).astype(o_ref.dtype)


def _conv_proj(a, w_in, conv_w, width, seq_len, bm=1024, bc=256):
    m, k = a.shape
    bm = min(bm, seq_len)
    bc = min(bc, width)
    nc = width // bc
    assert seq_len % bm == 0 and m % seq_len == 0 and width % bc == 0 and bm % 8 == 0

    def wcol(group):
        return pl.BlockSpec((k, bc), lambda i, j: (0, j + group * nc))

    return pl.pallas_call(
        functools.partial(_conv_proj_kernel, tiles_per_seq=seq_len // bm),
        out_shape=jax.ShapeDtypeStruct((m, width), jnp.bfloat16),
        grid=(m // bm, nc),
        in_specs=[pl.BlockSpec((bm, k), lambda i, j: (i, 0)),
                  wcol(0), wcol(1), wcol(2), wcol(3),
                  pl.BlockSpec((CONV_K, bc), lambda i, j: (0, j))],
        out_specs=pl.BlockSpec((bm, bc), lambda i, j: (i, j)),
        scratch_shapes=[pltpu.VMEM((nc, 8, bc), jnp.float32)],
        compiler_params=_cparams(2),
        name="conv_proj",
    )(a, w_in, w_in, w_in, w_in, conv_w)


def _t5_bucket_thresholds():
    d = np.arange(0, 4 * REL_MAX_DIST)
    max_exact = REL_BUCKETS // 2
    nf = np.maximum(d, 1).astype(np.float32)
    large = max_exact + (np.log(nf / np.float32(max_exact)) / np.float32(math.log(REL_MAX_DIST / max_exact))
                         * np.float32(REL_BUCKETS - max_exact)).astype(np.int32)
    large = np.minimum(large, REL_BUCKETS - 1)
    bucket = np.where(d < max_exact, d, large)
    assert np.all(np.diff(bucket) >= 0) and bucket[-1] == REL_BUCKETS - 1
    return [int(np.argmax(bucket >= b)) for b in range(REL_BUCKETS)]


_BUCKET_THR = _t5_bucket_thresholds()
_MOBA_BLOCK_SHIFT = MOBA_BLOCK.bit_length() - 1
assert 1 << _MOBA_BLOCK_SHIFT == MOBA_BLOCK


def _moba_kernel(rb_ref, q_ref, k_ref, v_ref, g_ref, o_ref, t0_ref, t1_ref, kaug_ref, *, nb):
    blk = MOBA_BLOCK
    s_len = nb * blk
    heads = range(MOBA_HEADS_PER_STEP)
    head_ids = [pl.program_id(0) * MOBA_HEADS_PER_STEP + hh for hh in heads]
    cols = [slice(hh * HEAD_DIM, (hh + 1) * HEAD_DIM) for hh in heads]
    qo = lax.broadcasted_iota(jnp.int32, (blk, blk), 0)
    ko = lax.broadcasted_iota(jnp.int32, (blk, blk), 1)

    @pl.when(pl.program_id(1) == 0)
    def _():
        def table(h, dist):
            t = jnp.full(dist.shape, rb_ref[h, REL_BUCKETS - 1], jnp.float32)
            for bkt in range(REL_BUCKETS - 2, -1, -1):
                t = jnp.where(dist < _BUCKET_THR[bkt + 1], rb_ref[h, bkt], t)
            return t * LOG2E
        for hh in heads:
            t0_ref[hh] = table(head_ids[hh], qo - ko)
            t1_ref[hh] = table(head_ids[hh], qo - ko + blk)

    rb_far = [rb_ref[h, REL_BUCKETS - 1] * LOG2E for h in head_ids]
    assert _BUCKET_THR[REL_BUCKETS - 1] <= blk + 1

    block_onehot = jnp.where(
        jnp.right_shift(lax.broadcasted_iota(jnp.int32, (s_len, HEAD_DIM), 0), _MOBA_BLOCK_SHIFT)
        == lax.broadcasted_iota(jnp.int32, (s_len, HEAD_DIM), 1), 1.0, 0.0).astype(jnp.bfloat16)
    kbars = []
    for hh in heads:
        kaug_ref[hh, :, 0:HEAD_DIM] = k_ref[0, :, cols[hh]]
        kaug_ref[hh, :, HEAD_DIM:2 * HEAD_DIM] = block_onehot
        kbar = jnp.concatenate(
            [jnp.mean(k_ref[0, j * blk:(j + 1) * blk, cols[hh]].astype(jnp.float32),
                      axis=0, keepdims=True) for j in range(nb)]
            + [jnp.zeros((16 - nb, HEAD_DIM), jnp.float32)], axis=0)
        kbars.append(_split_bf16(kbar))

    eye = jnp.where(qo == ko, 1.0, 0.0).astype(jnp.bfloat16)
    jidx = lax.broadcasted_iota(jnp.int32, (16, blk), 0)

    def logits(hh, i):
        q_i = q_ref[0, i * blk:(i + 1) * blk, cols[hh]]
        kbar_hi, kbar_lo = kbars[hh]
        nk = (i + 1) * blk
        if i > MOBA_TOPK:
            gate = (lax.dot_general(kbar_hi, q_i, _NT, preferred_element_type=jnp.float32)
                    + lax.dot_general(kbar_lo, q_i, _NT, preferred_element_type=jnp.float32))
            cnt = jnp.zeros((16, blk), jnp.float32)
            for jp in range(i):
                row = gate[jp:jp + 1, :]
                beats = jnp.where(row > gate, 1.0,
                                  jnp.where(row == gate, jnp.where(jidx > jp, 1.0, 0.0), 0.0))
                cnt = cnt + beats
            drop_t = jnp.where(cnt < MOBA_TOPK, 0.0, jnp.where(jidx < i, NEG_INF, 0.0))
            drop_t = jnp.concatenate([drop_t, jnp.zeros((128 - 16, blk), jnp.float32)], axis=0)
            drop = lax.dot_general(eye, drop_t.astype(jnp.bfloat16), _NT,
                                   preferred_element_type=jnp.float32)
            q_aug = jnp.concatenate([q_i, drop.astype(jnp.bfloat16)], axis=-1)
            return lax.dot_general(q_aug, kaug_ref[hh, 0:nk, :], _NT,
                                   preferred_element_type=jnp.float32)
        return lax.dot_general(q_i, k_ref[0, 0:nk, cols[hh]], _NT,
                               preferred_element_type=jnp.float32)

    def attend(hh, i, s):
        nk = (i + 1) * blk
        n_far = max(i - 1, 0) * blk
        lg_own = jnp.where(ko <= qo, s[:, i * blk:nk] + t0_ref[hh], NEG_INF)
        m = jnp.max(lg_own, axis=-1, keepdims=True)
        if i > 0:
            lg_prev = s[:, (i - 1) * blk:i * blk] + t1_ref[hh]
            m = jnp.maximum(m, jnp.max(lg_prev, axis=-1, keepdims=True))
        if n_far:
            m = jnp.maximum(m, jnp.max(s[:, 0:n_far], axis=-1, keepdims=True) + rb_far[hh])
        pieces = [jnp.exp2(lg_own - m)]
        if i > 0:
            pieces.insert(0, jnp.exp2(lg_prev - m))
        if n_far:
            pieces.insert(0, jnp.exp2(s[:, 0:n_far] - (m - rb_far[hh])))
        l = sum(jnp.sum(piece, axis=-1, keepdims=True) for piece in pieces)
        p = pieces[0] if len(pieces) == 1 else jnp.concatenate(pieces, axis=-1)
        o = jnp.dot(p.astype(jnp.bfloat16), v_ref[0, 0:nk, cols[hh]],
                    preferred_element_type=jnp.float32) / l
        rows = slice(i * blk, (i + 1) * blk)
        o_ref[0, rows, cols[hh]] = (o * _silu(g_ref[0, rows, cols[hh]])).astype(o_ref.dtype)

    items = [(hh, i) for hh in heads for i in range(nb)]
    s_next = logits(*items[0])
    for n, item in enumerate(items):
        s_cur, s_next = s_next, (logits(*items[n + 1]) if n + 1 < len(items) else None)
        attend(*item, s_cur)


def _moba(qkv, g, rel_bias_t, heads):
    bsz, s, _ = qkv.shape
    assert s % MOBA_BLOCK == 0
    nb = s // MOBA_BLOCK
    assert MOBA_TOPK < nb <= 16 and heads % MOBA_HEADS_PER_STEP == 0
    steps = heads // MOBA_HEADS_PER_STEP
    width = MOBA_HEADS_PER_STEP * HEAD_DIM

    def col(group):
        return pl.BlockSpec((1, s, width), lambda h, b: (b, 0, h + group * steps))

    return pl.pallas_call(
        functools.partial(_moba_kernel, nb=nb),
        out_shape=jax.ShapeDtypeStruct((bsz, s, heads * HEAD_DIM), jnp.bfloat16),
        grid=(steps, bsz),
        in_specs=[pl.BlockSpec(memory_space=pltpu.SMEM), col(0), col(1), col(2), col(0)],
        out_specs=col(0),
        scratch_shapes=[pltpu.VMEM((MOBA_HEADS_PER_STEP, MOBA_BLOCK, MOBA_BLOCK), jnp.float32),
                        pltpu.VMEM((MOBA_HEADS_PER_STEP, MOBA_BLOCK, MOBA_BLOCK), jnp.float32),
                        pltpu.VMEM((MOBA_HEADS_PER_STEP, s, 2 * HEAD_DIM), jnp.bfloat16)],
        compiler_params=_cparams(2),
        name="moba",
    )(rel_bias_t, qkv, qkv, qkv, g)


def _sb_kernel(q_ref, k_ref, v_ref, g_ref, o_ref, za_ref, zb_ref, acc_ref, car_ref, *, nq):
    t = SB_BLOCK
    tq = 2 * t
    tri = jnp.where(lax.broadcasted_iota(jnp.int32, (t, t), 0)
                    > lax.broadcasted_iota(jnp.int32, (t, t), 1), 1.0, 0.0).astype(jnp.bfloat16)

    def logits(qs, j0):
        return tuple(lax.dot_general(q, k_ref[0, pl.ds(j0, t), hh * HEAD_DIM:(hh + 1) * HEAD_DIM],
                                     _NT, preferred_element_type=jnp.float32)
                     for hh, q in enumerate(qs))

    def tiles(zs, j0, rows, diag):
        heads = range(len(zs))
        cols = [slice(hh * HEAD_DIM, (hh + 1) * HEAD_DIM) for hh in heads]
        if diag:
            strict = (lax.broadcasted_iota(jnp.int32, zs[0].shape, 1)
                      < lax.broadcasted_iota(jnp.int32, zs[0].shape, 0))
            zs = [jnp.where(strict, z, NEG_INF) for z in zs]
        zs = [jnp.minimum(z, SB_MAX_LOG2_ODDS) for z in zs]
        sps = [jnp.log(1.0 + jnp.exp2(z)) * LOG2E for z in zs]
        rems = [jnp.dot(sps[hh].astype(jnp.bfloat16), tri, preferred_element_type=jnp.float32)
                + car_ref[hh, rows, :] for hh in heads]
        ws = [jnp.exp2((zs[hh] - sps[hh]) - rems[hh]).astype(jnp.bfloat16) for hh in heads]
        for hh in heads:
            acc_ref[hh, rows, :] += jnp.dot(ws[hh], v_ref[0, pl.ds(j0, t), cols[hh]],
                                            preferred_element_type=jnp.float32)
            car_ref[hh, rows, :] = rems[hh][:, 0:1] + sps[hh][:, 0:1]

    def q_block(i, _):
        q0 = pl.multiple_of(i * tq, tq)
        q1 = pl.multiple_of(i * tq + t, t)
        heads = range(SB_HEADS_PER_STEP)
        cols = [slice(hh * HEAD_DIM, (hh + 1) * HEAD_DIM) for hh in heads]
        q_i = [q_ref[0, pl.ds(q0, tq), c] for c in cols]
        acc_ref[...] = jnp.zeros(acc_ref.shape, jnp.float32)
        car_ref[...] = jnp.zeros(car_ref.shape, jnp.float32)

        def past_j0(n):
            return pl.multiple_of(jnp.maximum(2 * i - 1 - n, 0) * t, t)

        def stage_logits(z_ref, n, rows):
            for hh, z in enumerate(logits([q[rows, :] for q in q_i], past_j0(n))):
                z_ref[hh, rows, :] = z

        zs_low = logits([q_ref[0, pl.ds(q1, t), c] for c in cols], q1)
        zs_diag = logits(q_i, q0)
        stage_logits(za_ref, 0, slice(0, tq))
        tiles(zs_low, q1, slice(t, tq), True)
        tiles(zs_diag, q0, slice(0, tq), True)

        def all_dead(rows):
            return (jnp.min(jnp.min(car_ref[:, rows, :], axis=0)) >= SB_DEAD_BITS).astype(jnp.int32)

        def walk(rows):
            def past_pair(carry):
                m = carry[0]
                stage_logits(zb_ref, 2 * m + 1, rows)
                tiles([za_ref[hh, rows, :] for hh in heads], past_j0(2 * m), rows, False)

                @pl.when(all_dead(rows) == 0)
                def _():
                    stage_logits(za_ref, 2 * m + 2, rows)
                    tiles([zb_ref[hh, rows, :] for hh in heads], past_j0(2 * m + 1), rows, False)

                return m + 1, all_dead(rows)

            lax.while_loop(lambda carry: (carry[0] < i) & (carry[1] == 0), past_pair,
                           (jnp.int32(0), jnp.int32(0)))

        lower_dead = all_dead(slice(t, tq))

        @pl.when(lower_dead == 1)
        def _():
            walk(slice(0, t))

        @pl.when(lower_dead == 0)
        def _():
            walk(slice(0, tq))
        acc = jnp.concatenate([acc_ref[hh] for hh in heads], axis=-1)
        o_ref[0, pl.ds(q0, tq), :] = (acc * _silu(g_ref[0, pl.ds(q0, tq), :])).astype(o_ref.dtype)
        return 0

    lax.fori_loop(0, nq, q_block, 0)


def _stick_breaking(qkv, g, heads):
    bsz, s, _ = qkv.shape
    assert s % (2 * SB_BLOCK) == 0 and heads % SB_HEADS_PER_STEP == 0
    steps = heads // SB_HEADS_PER_STEP
    width = SB_HEADS_PER_STEP * HEAD_DIM

    def col(group):
        return pl.BlockSpec((1, s, width), lambda h, b: (b, 0, h + group * steps))

    return pl.pallas_call(
        functools.partial(_sb_kernel, nq=s // (2 * SB_BLOCK)),
        out_shape=jax.ShapeDtypeStruct((bsz, s, heads * HEAD_DIM), jnp.bfloat16),
        grid=(steps, bsz),
        in_specs=[col(0), col(1), col(2), col(0)],
        out_specs=col(0),
        scratch_shapes=[pltpu.VMEM((SB_HEADS_PER_STEP, 2 * SB_BLOCK, SB_BLOCK), jnp.float32)] * 2
        + [pltpu.VMEM((SB_HEADS_PER_STEP, 2 * SB_BLOCK, HEAD_DIM), jnp.float32),
           pltpu.VMEM((SB_HEADS_PER_STEP, 2 * SB_BLOCK, 1), jnp.float32)],
        compiler_params=_cparams(2),
        name="stick_breaking",
    )(qkv, qkv, qkv, g)


def _xattn_kernel(q_ref, g_ref, k_ref, v_ref, o_ref, *, dh):
    scale = dh ** -0.5
    for h in range(XATTN_HEADS):
        cols = slice(h * dh, (h + 1) * dh)
        s = lax.dot_general(q_ref[0, :, cols], k_ref[0, :, cols], _NT,
                            preferred_element_type=jnp.float32) * scale
        m = jnp.max(s, axis=-1, keepdims=True)
        p = jnp.exp(s - m)
        l = jnp.sum(p, axis=-1, keepdims=True)
        o = jnp.dot(p.astype(jnp.bfloat16), v_ref[0, :, cols],
                    preferred_element_type=jnp.float32) / l
        o_ref[0, :, cols] = (o * _silu(g_ref[0, :, cols])).astype(o_ref.dtype)


def _xattn(q, g, kv, bm=512):
    bsz, s, w = q.shape
    n_mem = kv.shape[1]
    bm = min(bm, s)
    row = pl.BlockSpec((1, bm, w), lambda b, i: (b, i, 0))
    return pl.pallas_call(
        functools.partial(_xattn_kernel, dh=w // XATTN_HEADS),
        out_shape=jax.ShapeDtypeStruct((bsz, s, w), jnp.bfloat16),
        grid=(bsz, s // bm),
        in_specs=[row, row,
                  pl.BlockSpec((1, n_mem, w), lambda b, i: (b, 0, 0)),
                  pl.BlockSpec((1, n_mem, w), lambda b, i: (b, 0, 1))],
        out_specs=row,
        compiler_params=_cparams(2),
        name="xattn",
    )(q, g, kv, kv)


def _ln_proj_kernel(x_ref, y_ref, g_ref, b_ref, w_ref, xf_ref, q_ref, gate_ref, a0_ref, a1_ref, *,
                    q_blocks):
    i = pl.program_id(0)
    j = pl.program_id(1)
    chunk = x_ref.shape[0]
    a_refs = (a0_ref, a1_ref)

    def normalise(a_ref):
        o = _layer_norm_rows(DEEPNORM_ALPHA * x_ref[...] + y_ref[...], g_ref[...], b_ref[...])
        xf_ref[...] = o
        a_ref[pl.ds(pl.multiple_of(j * chunk, chunk), chunk), :] = o.astype(jnp.bfloat16)

    def emit(acc):
        @pl.when(j < q_blocks)
        def _():
            q_ref[...] = acc.astype(q_ref.dtype)

        @pl.when(j >= q_blocks)
        def _():
            gate_ref[...] = acc

    @pl.when(i == 0)
    def _():
        normalise(a_refs[0])

    for parity in (0, 1):
        @pl.when((i > 0) & (i % 2 == parity))
        def _():
            acc = jnp.dot(a_refs[1 - parity][...], w_ref[...], preferred_element_type=jnp.float32)
            normalise(a_refs[parity])
            emit(acc)


def _residual_ln_proj(x, y, g, b, w, q_cols, bm=1024, bn=256):
    m, d = x.shape
    n = w.shape[1]
    bm = min(bm, m)
    bn = math.gcd(bn, q_cols, n - q_cols)
    nj = n // bn
    chunk = bm // nj
    n_tiles = m // bm
    q_blocks = q_cols // bn
    assert m % bm == 0 and bm % nj == 0 and chunk % 16 == 0 and 0 < q_blocks < nj

    def chunk_idx(i, j):
        return (jnp.where(i < n_tiles, i * nj + j, n_tiles * nj - 1), 0)

    def out_idx(col):
        return lambda i, j: (jnp.maximum(i - 1, 0), jnp.where(i == 0, 0, col(j)))

    rows = pl.BlockSpec((chunk, d), chunk_idx)
    vec = pl.BlockSpec((1, d), lambda i, j: (0, 0))
    return pl.pallas_call(
        functools.partial(_ln_proj_kernel, q_blocks=q_blocks),
        out_shape=(jax.ShapeDtypeStruct((m, d), jnp.float32),
                   jax.ShapeDtypeStruct((m, q_cols), jnp.bfloat16),
                   jax.ShapeDtypeStruct((m, n - q_cols), jnp.float32)),
        grid=(n_tiles + 1, nj),
        in_specs=[rows, rows, vec, vec, pl.BlockSpec((d, bn), lambda i, j: (0, j))],
        out_specs=(rows,
                   pl.BlockSpec((bm, bn), out_idx(lambda j: jnp.minimum(j, q_blocks - 1))),
                   pl.BlockSpec((bm, bn), out_idx(lambda j: jnp.maximum(j - q_blocks, 0)))),
        scratch_shapes=[pltpu.VMEM((bm, d), jnp.bfloat16)] * 2,
        compiler_params=_cparams(2),
        name="residual_ln_proj",
    )(x, y, g.reshape(1, d), b.reshape(1, d), w)


def _memory_block(x_f32, y, g1, b1, mem_bf16, w_xq, w_xkv, w_xo, g2, b2, bsz, s, emit_bf16):
    xw = w_xo.shape[0]
    x_f32, q, g = _residual_ln_proj(x_f32, y, g1, b1, w_xq, xw)
    kv = _matmul(mem_bf16, w_xkv, 0, 2 * xw, jnp.bfloat16)
    n_mem = mem_bf16.shape[0] // bsz
    o = _xattn(q.reshape(bsz, s, xw), g.reshape(bsz, s, xw), kv.reshape(bsz, n_mem, 2 * xw))
    return _matmul_residual_ln(o.reshape(bsz * s, xw), w_xo, x_f32, g2, b2, emit_bf16)


def kernel(x, mem, rel_bias, w_in_0, conv_w_0, w_out_0, ln1_g_0, ln1_b_0, w_xq_0, w_xkv_0, w_xo_0, ln2_g_0, ln2_b_0, w_in_1, w_out_1, ln1_g_1, ln1_b_1, w_xq_1, w_xkv_1, w_xo_1, ln2_g_1, ln2_b_1):
    bsz, s, d = x.shape
    bf = jnp.bfloat16
    xf = x.reshape(bsz * s, d)
    xb = xf.astype(bf)
    memb = mem.reshape(-1, d).astype(bf)

    cw = conv_w_0.shape[1]
    mw = w_in_0.shape[1] // 4 - cw
    heads = mw // HEAD_DIM
    w_in = w_in_0.astype(bf)
    y_a = _conv_proj(xb, w_in, conv_w_0, cw, s)
    qkv = _matmul(xb, w_in, 4 * cw, 3 * mw, bf, scaled_cols=mw, scale=HEAD_DIM ** -0.5 * LOG2E)
    g_b = _matmul(xb, w_in, 4 * cw + 3 * mw, mw, jnp.float32)
    y_b = _moba(qkv.reshape(bsz, s, 3 * mw), g_b.reshape(bsz, s, mw), rel_bias.T, heads)
    y = _matmul_cat(y_a, y_b.reshape(bsz * s, mw), w_out_0.astype(bf), jnp.float32)
    xf, xb = _memory_block(xf, y, ln1_g_0, ln1_b_0, memb, w_xq_0.astype(bf), w_xkv_0.astype(bf),
                           w_xo_0.astype(bf), ln2_g_0, ln2_b_0, bsz, s, True)

    sw = w_out_1.shape[0]
    w_in = w_in_1.astype(bf)
    qkv = _matmul(xb, w_in, 0, 3 * sw, bf, scaled_cols=sw, scale=HEAD_DIM ** -0.5 * LOG2E)
    g_c = _matmul(xb, w_in, 3 * sw, sw, jnp.float32)
    y_c = _stick_breaking(qkv.reshape(bsz, s, 3 * sw), g_c.reshape(bsz, s, sw), sw // HEAD_DIM)
    y = _matmul(y_c.reshape(bsz * s, sw), w_out_1.astype(bf), 0, d, jnp.float32)
    xf, _ = _memory_block(xf, y, ln1_g_1, ln1_b_1, memb, w_xq_1.astype(bf), w_xkv_1.astype(bf),
                          w_xo_1.astype(bf), ln2_g_1, ln2_b_1, bsz, s, False)
    return xf.reshape(bsz, s, d)
```

```python
import functools
import math

import numpy as np
import jax
import jax.numpy as jnp
from jax import lax
from jax.experimental import pallas as pl
from jax.experimental.pallas import tpu as pltpu

DEPTH = 2
HEAD_DIM = 128
CONV_K = 3
MOBA_BLOCK = 256
MOBA_TOPK = 3
XATTN_HEADS = 4
REL_BUCKETS = 32
REL_MAX_DIST = 128
LN_EPS = 1e-5
NEG_INF = -1e30
DEEPNORM_ALPHA = (2 * DEPTH) ** 0.25
LOG2E = math.log2(math.e)

MOBA_HEADS_PER_STEP = 2
SB_BLOCK = 256
SB_HEADS_PER_STEP = 4
SB_MAX_LOG2_ODDS = 60.0
SB_DEAD_BITS = 160.0
V7X_VMEM_BYTES = 64 * 1024 * 1024
VMEM_LIMIT = V7X_VMEM_BYTES - 8 * 1024 * 1024

_NT = (((1,), (1,)), ((), ()))


def _cparams(n_axes):
    return pltpu.CompilerParams(dimension_semantics=("arbitrary",) * n_axes,
                                vmem_limit_bytes=VMEM_LIMIT)


def _silu(g):
    return g * jax.nn.sigmoid(g)


def _split_bf16(x):
    hi = x.astype(jnp.bfloat16)
    lo = (x - hi.astype(jnp.float32)).astype(jnp.bfloat16)
    return hi, lo


def _mm_kernel(a_ref, w_ref, o_ref, *, scaled_blocks, scale):
    acc = jnp.dot(a_ref[...], w_ref[...], preferred_element_type=jnp.float32)
    if scaled_blocks:
        acc = acc * jnp.where(pl.program_id(1) < scaled_blocks, scale, 1.0)
    o_ref[...] = acc.astype(o_ref.dtype)


def _matmul(a, w, n_start, n_size, out_dtype, bm=1024, bn=1024, scaled_cols=0, scale=1.0):
    m, k = a.shape
    bm = min(bm, m)
    bn = math.gcd(math.gcd(math.gcd(bn, n_size), n_start), scaled_cols)
    assert m % bm == 0 and bn % 128 == 0
    joff = n_start // bn
    return pl.pallas_call(
        functools.partial(_mm_kernel, scaled_blocks=scaled_cols // bn, scale=scale),
        out_shape=jax.ShapeDtypeStruct((m, n_size), out_dtype),
        grid=(m // bm, n_size // bn),
        in_specs=[pl.BlockSpec((bm, k), lambda i, j: (i, 0)),
                  pl.BlockSpec((k, bn), lambda i, j: (0, j + joff))],
        out_specs=pl.BlockSpec((bm, bn), lambda i, j: (i, j)),
        compiler_params=_cparams(2),
        name="matmul",
    )(a, w)


def _mm2_kernel(a1_ref, a2_ref, w_ref, o_ref):
    k1 = a1_ref.shape[1]
    acc = jnp.dot(a1_ref[...], w_ref[0:k1, :], preferred_element_type=jnp.float32)
    acc = acc + jnp.dot(a2_ref[...], w_ref[k1:, :], preferred_element_type=jnp.float32)
    o_ref[...] = acc.astype(o_ref.dtype)


def _matmul_cat(a1, a2, w, out_dtype, bm=1024, bn=1024):
    m, k1 = a1.shape
    k2 = a2.shape[1]
    n = w.shape[1]
    bm = min(bm, m)
    bn = min(bn, n)
    assert m % bm == 0 and n % bn == 0 and w.shape[0] == k1 + k2
    return pl.pallas_call(
        _mm2_kernel,
        out_shape=jax.ShapeDtypeStruct((m, n), out_dtype),
        grid=(m // bm, n // bn),
        in_specs=[pl.BlockSpec((bm, k1), lambda i, j: (i, 0)),
                  pl.BlockSpec((bm, k2), lambda i, j: (i, 0)),
                  pl.BlockSpec((k1 + k2, bn), lambda i, j: (0, j))],
        out_specs=pl.BlockSpec((bm, bn), lambda i, j: (i, j)),
        compiler_params=_cparams(2),
        name="matmul_cat",
    )(a1, a2, w)


def _layer_norm_rows(r, g, b):
    mu = jnp.mean(r, axis=-1, keepdims=True)
    d = r - mu
    var = jnp.mean(d * d, axis=-1, keepdims=True)
    return d * lax.rsqrt(var + LN_EPS) * g + b


def _mm_ln_pipelined_kernel(a_ref, w_ref, x_ref, g_ref, b_ref, of_ref, *rest):
    ob_ref = rest[0] if len(rest) == 3 else None
    y_refs = rest[-2:]
    i = pl.program_id(0)

    def multiply(y_ref):
        y_ref[...] = jnp.dot(a_ref[...], w_ref[...], preferred_element_type=jnp.float32)

    def normalise(y_ref):
        o = _layer_norm_rows(DEEPNORM_ALPHA * x_ref[...] + y_ref[...], g_ref[...], b_ref[...])
        of_ref[...] = o
        if ob_ref is not None:
            ob_ref[...] = o.astype(jnp.bfloat16)

    @pl.when(i == 0)
    def _():
        multiply(y_refs[0])

    for parity in (0, 1):
        @pl.when((i > 0) & (i % 2 == parity))
        def _():
            multiply(y_refs[parity])
            normalise(y_refs[1 - parity])


def _matmul_residual_ln(a, w, x, g, b, emit_bf16, bm=256):
    m, k = a.shape
    d = w.shape[1]
    bm = min(bm, m)
    n_tiles = m // bm
    lagged = pl.BlockSpec((bm, d), lambda i: (jnp.maximum(i - 1, 0), 0))
    vec = pl.BlockSpec((1, d), lambda i: (0, 0))
    out_shape = [jax.ShapeDtypeStruct((m, d), jnp.float32)]
    if emit_bf16:
        out_shape.append(jax.ShapeDtypeStruct((m, d), jnp.bfloat16))
    out = pl.pallas_call(
        _mm_ln_pipelined_kernel,
        out_shape=tuple(out_shape),
        grid=(n_tiles + 1,),
        in_specs=[pl.BlockSpec((bm, k), lambda i: (jnp.minimum(i, n_tiles - 1), 0)),
                  pl.BlockSpec((k, d), lambda i: (0, 0)), lagged, vec, vec],
        out_specs=tuple([lagged] * len(out_shape)),
        scratch_shapes=[pltpu.VMEM((bm, d), jnp.float32)] * 2,
        compiler_params=_cparams(1),
        name="matmul_residual_ln",
    )(a, w, x, g.reshape(1, d), b.reshape(1, d))
    return out if emit_bf16 else (out[0], None)


def _conv_proj_kernel(a_ref, wh_ref, wb_ref, wc_ref, wg_ref, cw_ref, o_ref, tail_ref, *,
                      tiles_per_seq):
    i = pl.program_id(0)
    j = pl.program_id(1)
    a = a_ref[...]
    u = (jnp.dot(a, wc_ref[...], preferred_element_type=jnp.float32)
         * jnp.dot(a, wh_ref[...], preferred_element_type=jnp.float32))
    bm = u.shape[0]

    @pl.when(i % tiles_per_seq == 0)
    def _():
        tail_ref[j] = jnp.zeros(tail_ref.shape[1:], jnp.float32)

    prev = tail_ref[j]
    tail_ref[j] = u[bm - 8:bm, :]
    row = lax.broadcasted_iota(jnp.int32, u.shape, 0)
    u1 = jnp.where(row == 0, prev[7:8, :], pltpu.roll(u, 1, 0))
    u2 = jnp.where(row == 0, prev[6:7, :],
                   jnp.where(row == 1, prev[7:8, :], pltpu.roll(u, 2, 0)))
    w = cw_ref[...]
    conv = w[0:1, :] * u2 + w[1:2, :] * u1 + w[2:3, :] * u
    b = jnp.dot(a, wb_ref[...], preferred_element_type=jnp.float32)
    g = jnp.dot(a, wg_ref[...], preferred_element_type=jnp.float32)
    o_ref[...] = (b * conv * _silu(g)).astype(o_ref.dtype)


def _conv_proj(a, w_in, conv_w, width, seq_len, bm=1024, bc=256):
    m, k = a.shape
    bm = min(bm, seq_len)
    bc = min(bc, width)
    nc = width // bc
    assert seq_len % bm == 0 and m % seq_len == 0 and width % bc == 0 and bm % 8 == 0

    def wcol(group):
        return pl.BlockSpec((k, bc), lambda i, j: (0, j + group * nc))

    return pl.pallas_call(
        functools.partial(_conv_proj_kernel, tiles_per_seq=seq_len // bm),
        out_shape=jax.ShapeDtypeStruct((m, width), jnp.bfloat16),
        grid=(m // bm, nc),
        in_specs=[pl.BlockSpec((bm, k), lambda i, j: (i, 0)),
                  wcol(0), wcol(1), wcol(2), wcol(3),
                  pl.BlockSpec((CONV_K, bc), lambda i, j: (0, j))],
        out_specs=pl.BlockSpec((bm, bc), lambda i, j: (i, j)),
        scratch_shapes=[pltpu.VMEM((nc, 8, bc), jnp.float32)],
        compiler_params=_cparams(2),
        name="conv_proj",
    )(a, w_in, w_in, w_in, w_in, conv_w)


def _t5_bucket_thresholds():
    d = np.arange(0, 4 * REL_MAX_DIST)
    max_exact = REL_BUCKETS // 2
    nf = np.maximum(d, 1).astype(np.float32)
    large = max_exact + (np.log(nf / np.float32(max_exact)) / np.float32(math.log(REL_MAX_DIST / max_exact))
                         * np.float32(REL_BUCKETS - max_exact)).astype(np.int32)
    large = np.minimum(large, REL_BUCKETS - 1)
    bucket = np.where(d < max_exact, d, large)
    assert np.all(np.diff(bucket) >= 0) and bucket[-1] == REL_BUCKETS - 1
    return [int(np.argmax(bucket >= b)) for b in range(REL_BUCKETS)]


_BUCKET_THR = _t5_bucket_thresholds()
_MOBA_BLOCK_SHIFT = MOBA_BLOCK.bit_length() - 1
assert 1 << _MOBA_BLOCK_SHIFT == MOBA_BLOCK


def _moba_kernel(rb_ref, q_ref, k_ref, v_ref, g_ref, o_ref, t0_ref, t1_ref, kaug_ref, *, nb):
    blk = MOBA_BLOCK
    s_len = nb * blk
    heads = range(MOBA_HEADS_PER_STEP)
    head_ids = [pl.program_id(0) * MOBA_HEADS_PER_STEP + hh for hh in heads]
    cols = [slice(hh * HEAD_DIM, (hh + 1) * HEAD_DIM) for hh in heads]
    qo = lax.broadcasted_iota(jnp.int32, (blk, blk), 0)
    ko = lax.broadcasted_iota(jnp.int32, (blk, blk), 1)

    @pl.when(pl.program_id(1) == 0)
    def _():
        def table(h, dist):
            t = jnp.full(dist.shape, rb_ref[h, REL_BUCKETS - 1], jnp.float32)
            for bkt in range(REL_BUCKETS - 2, -1, -1):
                t = jnp.where(dist < _BUCKET_THR[bkt + 1], rb_ref[h, bkt], t)
            return t * LOG2E
        for hh in heads:
            t0_ref[hh] = table(head_ids[hh], qo - ko)
            t1_ref[hh] = table(head_ids[hh], qo - ko + blk)

    rb_far = [rb_ref[h, REL_BUCKETS - 1] * LOG2E for h in head_ids]
    assert _BUCKET_THR[REL_BUCKETS - 1] <= blk + 1

    block_onehot = jnp.where(
        jnp.right_shift(lax.broadcasted_iota(jnp.int32, (s_len, HEAD_DIM), 0), _MOBA_BLOCK_SHIFT)
        == lax.broadcasted_iota(jnp.int32, (s_len, HEAD_DIM), 1), 1.0, 0.0).astype(jnp.bfloat16)
    kbars = []
    for hh in heads:
        kaug_ref[hh, :, 0:HEAD_DIM] = k_ref[0, :, cols[hh]]
        kaug_ref[hh, :, HEAD_DIM:2 * HEAD_DIM] = block_onehot
        kbar = jnp.concatenate(
            [jnp.mean(k_ref[0, j * blk:(j + 1) * blk, cols[hh]].astype(jnp.float32),
                      axis=0, keepdims=True) for j in range(nb)]
            + [jnp.zeros((16 - nb, HEAD_DIM), jnp.float32)], axis=0)
        kbars.append(_split_bf16(kbar))

    eye = jnp.where(qo == ko, 1.0, 0.0).astype(jnp.bfloat16)
    jidx = lax.broadcasted_iota(jnp.int32, (16, blk), 0)

    def logits(hh, i):
        q_i = q_ref[0, i * blk:(i + 1) * blk, cols[hh]]
        kbar_hi, kbar_lo = kbars[hh]
        nk = (i + 1) * blk
        if i > MOBA_TOPK:
            gate = (lax.dot_general(kbar_hi, q_i, _NT, preferred_element_type=jnp.float32)
                    + lax.dot_general(kbar_lo, q_i, _NT, preferred_element_type=jnp.float32))
            cnt = jnp.zeros((16, blk), jnp.float32)
            for jp in range(i):
                row = gate[jp:jp + 1, :]
                beats = jnp.where(row > gate, 1.0,
                                  jnp.where(row == gate, jnp.where(jidx > jp, 1.0, 0.0), 0.0))
                cnt = cnt + beats
            drop_t = jnp.where(cnt < MOBA_TOPK, 0.0, jnp.where(jidx < i, NEG_INF, 0.0))
            drop_t = jnp.concatenate([drop_t, jnp.zeros((128 - 16, blk), jnp.float32)], axis=0)
            drop = lax.dot_general(eye, drop_t.astype(jnp.bfloat16), _NT,
                                   preferred_element_type=jnp.float32)
            q_aug = jnp.concatenate([q_i, drop.astype(jnp.bfloat16)], axis=-1)
            return lax.dot_general(q_aug, kaug_ref[hh, 0:nk, :], _NT,
                                   preferred_element_type=jnp.float32)
        return lax.dot_general(q_i, k_ref[0, 0:nk, cols[hh]], _NT,
                               preferred_element_type=jnp.float32)

    def attend(hh, i, s):
        nk = (i + 1) * blk
        n_far = max(i - 1, 0) * blk
        lg_own = jnp.where(ko <= qo, s[:, i * blk:nk] + t0_ref[hh], NEG_INF)
        m = jnp.max(lg_own, axis=-1, keepdims=True)
        if i > 0:
            lg_prev = s[:, (i - 1) * blk:i * blk] + t1_ref[hh]
            m = jnp.maximum(m, jnp.max(lg_prev, axis=-1, keepdims=True))
        if n_far:
            m = jnp.maximum(m, jnp.max(s[:, 0:n_far], axis=-1, keepdims=True) + rb_far[hh])
        pieces = [jnp.exp2(lg_own - m)]
        if i > 0:
            pieces.insert(0, jnp.exp2(lg_prev - m))
        if n_far:
            pieces.insert(0, jnp.exp2(s[:, 0:n_far] - (m - rb_far[hh])))
        l = sum(jnp.sum(piece, axis=-1, keepdims=True) for piece in pieces)
        p = pieces[0] if len(pieces) == 1 else jnp.concatenate(pieces, axis=-1)
        o = jnp.dot(p.astype(jnp.bfloat16), v_ref[0, 0:nk, cols[hh]],
                    preferred_element_type=jnp.float32) / l
        rows = slice(i * blk, (i + 1) * blk)
        o_ref[0, rows, cols[hh]] = (o * _silu(g_ref[0, rows, cols[hh]])).astype(o_ref.dtype)

    items = [(hh, i) for hh in heads for i in range(nb)]
    s_next = logits(*items[0])
    for n, item in enumerate(items):
        s_cur, s_next = s_next, (logits(*items[n + 1]) if n + 1 < len(items) else None)
        attend(*item, s_cur)


def _moba(qkv, g, rel_bias_t, heads):
    bsz, s, _ = qkv.shape
    assert s % MOBA_BLOCK == 0
    nb = s // MOBA_BLOCK
    assert MOBA_TOPK < nb <= 16 and heads % MOBA_HEADS_PER_STEP == 0
    steps = heads // MOBA_HEADS_PER_STEP
    width = MOBA_HEADS_PER_STEP * HEAD_DIM

    def col(group):
        return pl.BlockSpec((1, s, width), lambda h, b: (b, 0, h + group * steps))

    return pl.pallas_call(
        functools.partial(_moba_kernel, nb=nb),
        out_shape=jax.ShapeDtypeStruct((bsz, s, heads * HEAD_DIM), jnp.bfloat16),
        grid=(steps, bsz),
        in_specs=[pl.BlockSpec(memory_space=pltpu.SMEM), col(0), col(1), col(2), col(0)],
        out_specs=col(0),
        scratch_shapes=[pltpu.VMEM((MOBA_HEADS_PER_STEP, MOBA_BLOCK, MOBA_BLOCK), jnp.float32),
                        pltpu.VMEM((MOBA_HEADS_PER_STEP, MOBA_BLOCK, MOBA_BLOCK), jnp.float32),
                        pltpu.VMEM((MOBA_HEADS_PER_STEP, s, 2 * HEAD_DIM), jnp.bfloat16)],
        compiler_params=_cparams(2),
        name="moba",
    )(rel_bias_t, qkv, qkv, qkv, g)


def _sb_kernel(q_ref, k_ref, v_ref, g_ref, o_ref, za_ref, zb_ref, acc_ref, car_ref, *, nq):
    t = SB_BLOCK
    tq = 2 * t
    tri = jnp.where(lax.broadcasted_iota(jnp.int32, (t, t), 0)
                    > lax.broadcasted_iota(jnp.int32, (t, t), 1), 1.0, 0.0).astype(jnp.bfloat16)

    def logits(qs, j0):
        return tuple(lax.dot_general(q, k_ref[0, pl.ds(j0, t), hh * HEAD_DIM:(hh + 1) * HEAD_DIM],
                                     _NT, preferred_element_type=jnp.float32)
                     for hh, q in enumerate(qs))

    def tiles(zs, j0, rows, diag):
        heads = range(len(zs))
        cols = [slice(hh * HEAD_DIM, (hh + 1) * HEAD_DIM) for hh in heads]
        if diag:
            strict = (lax.broadcasted_iota(jnp.int32, zs[0].shape, 1)
                      < lax.broadcasted_iota(jnp.int32, zs[0].shape, 0))
            zs = [jnp.where(strict, z, NEG_INF) for z in zs]
        zs = [jnp.minimum(z, SB_MAX_LOG2_ODDS) for z in zs]
        sps = [jnp.log(1.0 + jnp.exp2(z)) * LOG2E for z in zs]
        rems = [jnp.dot(sps[hh].astype(jnp.bfloat16), tri, preferred_element_type=jnp.float32)
                + car_ref[hh, rows, :] for hh in heads]
        ws = [jnp.exp2((zs[hh] - sps[hh]) - rems[hh]).astype(jnp.bfloat16) for hh in heads]
        for hh in heads:
            acc_ref[hh, rows, :] += jnp.dot(ws[hh], v_ref[0, pl.ds(j0, t), cols[hh]],
                                            preferred_element_type=jnp.float32)
            car_ref[hh, rows, :] = rems[hh][:, 0:1] + sps[hh][:, 0:1]

    def q_block(i, _):
        q0 = pl.multiple_of(i * tq, tq)
        q1 = pl.multiple_of(i * tq + t, t)
        heads = range(SB_HEADS_PER_STEP)
        cols = [slice(hh * HEAD_DIM, (hh + 1) * HEAD_DIM) for hh in heads]
        q_i = [q_ref[0, pl.ds(q0, tq), c] for c in cols]
        acc_ref[...] = jnp.zeros(acc_ref.shape, jnp.float32)
        car_ref[...] = jnp.zeros(car_ref.shape, jnp.float32)

        def past_j0(n):
            return pl.multiple_of(jnp.maximum(2 * i - 1 - n, 0) * t, t)

        def stage_logits(z_ref, n, rows):
            for hh, z in enumerate(logits([q[rows, :] for q in q_i], past_j0(n))):
                z_ref[hh, rows, :] = z

        zs_low = logits([q_ref[0, pl.ds(q1, t), c] for c in cols], q1)
        zs_diag = logits(q_i, q0)
        stage_logits(za_ref, 0, slice(0, tq))
        tiles(zs_low, q1, slice(t, tq), True)
        tiles(zs_diag, q0, slice(0, tq), True)

        def all_dead(rows):
            return (jnp.min(jnp.min(car_ref[:, rows, :], axis=0)) >= SB_DEAD_BITS).astype(jnp.int32)

        def walk(rows):
            def past_pair(carry):
                m = carry[0]
                stage_logits(zb_ref, 2 * m + 1, rows)
                tiles([za_ref[hh, rows, :] for hh in heads], past_j0(2 * m), rows, False)

                @pl.when(all_dead(rows) == 0)
                def _():
                    stage_logits(za_ref, 2 * m + 2, rows)
                    tiles([zb_ref[hh, rows, :] for hh in heads], past_j0(2 * m + 1), rows, False)

                return m + 1, all_dead(rows)

            lax.while_loop(lambda carry: (carry[0] < i) & (carry[1] == 0), past_pair,
                           (jnp.int32(0), jnp.int32(0)))

        lower_dead = all_dead(slice(t, tq))

        @pl.when(lower_dead == 1)
        def _():
            walk(slice(0, t))

        @pl.when(lower_dead == 0)
        def _():
            walk(slice(0, tq))
        acc = jnp.concatenate([acc_ref[hh] for hh in heads], axis=-1)
        o_ref[0, pl.ds(q0, tq), :] = (acc * _silu(g_ref[0, pl.ds(q0, tq), :])).astype(o_ref.dtype)
        return 0

    lax.fori_loop(0, nq, q_block, 0)


def _stick_breaking(qkv, g, heads):
    bsz, s, _ = qkv.shape
    assert s % (2 * SB_BLOCK) == 0 and heads % SB_HEADS_PER_STEP == 0
    steps = heads // SB_HEADS_PER_STEP
    width = SB_HEADS_PER_STEP * HEAD_DIM

    def col(group):
        return pl.BlockSpec((1, s, width), lambda h, b: (b, 0, h + group * steps))

    return pl.pallas_call(
        functools.partial(_sb_kernel, nq=s // (2 * SB_BLOCK)),
        out_shape=jax.ShapeDtypeStruct((bsz, s, heads * HEAD_DIM), jnp.bfloat16),
        grid=(steps, bsz),
        in_specs=[col(0), col(1), col(2), col(0)],
        out_specs=col(0),
        scratch_shapes=[pltpu.VMEM((SB_HEADS_PER_STEP, 2 * SB_BLOCK, SB_BLOCK), jnp.float32)] * 2
        + [pltpu.VMEM((SB_HEADS_PER_STEP, 2 * SB_BLOCK, HEAD_DIM), jnp.float32),
           pltpu.VMEM((SB_HEADS_PER_STEP, 2 * SB_BLOCK, 1), jnp.float32)],
        compiler_params=_cparams(2),
        name="stick_breaking",
    )(qkv, qkv, qkv, g)


def _xattn_kernel(q_ref, g_ref, k_ref, v_ref, o_ref, *, dh):
    scale = dh ** -0.5
    for h in range(XATTN_HEADS):
        cols = slice(h * dh, (h + 1) * dh)
        s = lax.dot_general(q_ref[0, :, cols], k_ref[0, :, cols], _NT,
                            preferred_element_type=jnp.float32) * scale
        m = jnp.max(s, axis=-1, keepdims=True)
        p = jnp.exp(s - m)
        l = jnp.sum(p, axis=-1, keepdims=True)
        o = jnp.dot(p.astype(jnp.bfloat16), v_ref[0, :, cols],
                    preferred_element_type=jnp.float32) / l
        o_ref[0, :, cols] = (o * _silu(g_ref[0, :, cols])).astype(o_ref.dtype)


def _xattn(q, g, kv, bm=512):
    bsz, s, w = q.shape
    n_mem = kv.shape[1]
    bm = min(bm, s)
    row = pl.BlockSpec((1, bm, w), lambda b, i: (b, i, 0))
    return pl.pallas_call(
        functools.partial(_xattn_kernel, dh=w // XATTN_HEADS),
        out_shape=jax.ShapeDtypeStruct((bsz, s, w), jnp.bfloat16),
        grid=(bsz, s // bm),
        in_specs=[row, row,
                  pl.BlockSpec((1, n_mem, w), lambda b, i: (b, 0, 0)),
                  pl.BlockSpec((1, n_mem, w), lambda b, i: (b, 0, 1))],
        out_specs=row,
        compiler_params=_cparams(2),
        name="xattn",
    )(q, g, kv, kv)


def _ln_proj_kernel(x_ref, y_ref, g_ref, b_ref, w_ref, xf_ref, q_ref, gate_ref, a0_ref, a1_ref, *,
                    q_blocks):
    i = pl.program_id(0)
    j = pl.program_id(1)
    chunk = x_ref.shape[0]
    a_refs = (a0_ref, a1_ref)

    def normalise(a_ref):
        o = _layer_norm_rows(DEEPNORM_ALPHA * x_ref[...] + y_ref[...], g_ref[...], b_ref[...])
        xf_ref[...] = o
        a_ref[pl.ds(pl.multiple_of(j * chunk, chunk), chunk), :] = o.astype(jnp.bfloat16)

    def emit(acc):
        @pl.when(j < q_blocks)
        def _():
            q_ref[...] = acc.astype(q_ref.dtype)

        @pl.when(j >= q_blocks)
        def _():
            gate_ref[...] = acc

    @pl.when(i == 0)
    def _():
        normalise(a_refs[0])

    for parity in (0, 1):
        @pl.when((i > 0) & (i % 2 == parity))
        def _():
            acc = jnp.dot(a_refs[1 - parity][...], w_ref[...], preferred_element_type=jnp.float32)
            normalise(a_refs[parity])
            emit(acc)


def _residual_ln_proj(x, y, g, b, w, q_cols, bm=1024, bn=256):
    m, d = x.shape
    n = w.shape[1]
    bm = min(bm, m)
    bn = math.gcd(bn, q_cols, n - q_cols)
    nj = n // bn
    chunk = bm // nj
    n_tiles = m // bm
    q_blocks = q_cols // bn
    assert m % bm == 0 and bm % nj == 0 and chunk % 16 == 0 and 0 < q_blocks < nj

    def chunk_idx(i, j):
        return (jnp.where(i < n_tiles, i * nj + j, n_tiles * nj - 1), 0)

    def out_idx(col):
        return lambda i, j: (jnp.maximum(i - 1, 0), jnp.where(i == 0, 0, col(j)))

    rows = pl.BlockSpec((chunk, d), chunk_idx)
    vec = pl.BlockSpec((1, d), lambda i, j: (0, 0))
    return pl.pallas_call(
        functools.partial(_ln_proj_kernel, q_blocks=q_blocks),
        out_shape=(jax.ShapeDtypeStruct((m, d), jnp.float32),
                   jax.ShapeDtypeStruct((m, q_cols), jnp.bfloat16),
                   jax.ShapeDtypeStruct((m, n - q_cols), jnp.float32)),
        grid=(n_tiles + 1, nj),
        in_specs=[rows, rows, vec, vec, pl.BlockSpec((d, bn), lambda i, j: (0, j))],
        out_specs=(rows,
                   pl.BlockSpec((bm, bn), out_idx(lambda j: jnp.minimum(j, q_blocks - 1))),
                   pl.BlockSpec((bm, bn), out_idx(lambda j: jnp.maximum(j - q_blocks, 0)))),
        scratch_shapes=[pltpu.VMEM((bm, d), jnp.bfloat16)] * 2,
        compiler_params=_cparams(2),
        name="residual_ln_proj",
    )(x, y, g.reshape(1, d), b.reshape(1, d), w)


def _memory_block(x_f32, y, g1, b1, mem_bf16, w_xq, w_xkv, w_xo, g2, b2, bsz, s, emit_bf16):
    xw = w_xo.shape[0]
    x_f32, q, g = _residual_ln_proj(x_f32, y, g1, b1, w_xq, xw)
    kv = _matmul(mem_bf16, w_xkv, 0, 2 * xw, jnp.bfloat16)
    n_mem = mem_bf16.shape[0] // bsz
    o = _xattn(q.reshape(bsz, s, xw), g.reshape(bsz, s, xw), kv.reshape(bsz, n_mem, 2 * xw))
    return _matmul_residual_ln(o.reshape(bsz * s, xw), w_xo, x_f32, g2, b2, emit_bf16)


def kernel(x, mem, rel_bias, w_in_0, conv_w_0, w_out_0, ln1_g_0, ln1_b_0, w_xq_0, w_xkv_0, w_xo_0, ln2_g_0, ln2_b_0, w_in_1, w_out_1, ln1_g_1, ln1_b_1, w_xq_1, w_xkv_1, w_xo_1, ln2_g_1, ln2_b_1):
    bsz, s, d = x.shape
    bf = jnp.bfloat16
    xf = x.reshape(bsz * s, d)
    xb = xf.astype(bf)
    memb = mem.reshape(-1, d).astype(bf)

    cw = conv_w_0.shape[1]
    mw = w_in_0.shape[1] // 4 - cw
    heads = mw // HEAD_DIM
    w_in = w_in_0.astype(bf)
    y_a = _conv_proj(xb, w_in, conv_w_0, cw, s)
    qkv = _matmul(xb, w_in, 4 * cw, 3 * mw, bf, scaled_cols=mw, scale=HEAD_DIM ** -0.5 * LOG2E)
    g_b = _matmul(xb, w_in, 4 * cw + 3 * mw, mw, jnp.float32)
    y_b = _moba(qkv.reshape(bsz, s, 3 * mw), g_b.reshape(bsz, s, mw), rel_bias.T, heads)
    y = _matmul_cat(y_a, y_b.reshape(bsz * s, mw), w_out_0.astype(bf), jnp.float32)
    xf, xb = _memory_block(xf, y, ln1_g_0, ln1_b_0, memb, w_xq_0.astype(bf), w_xkv_0.astype(bf),
                           w_xo_0.astype(bf), ln2_g_0, ln2_b_0, bsz, s, True)

    sw = w_out_1.shape[0]
    w_in = w_in_1.astype(bf)
    qkv = _matmul(xb, w_in, 0, 3 * sw, bf, scaled_cols=sw, scale=HEAD_DIM ** -0.5 * LOG2E)
    g_c = _matmul(xb, w_in, 3 * sw, sw, jnp.float32)
    y_c = _stick_breaking(qkv.reshape(bsz, s, 3 * sw), g_c.reshape(bsz, s, sw), sw // HEAD_DIM)
    y = _matmul(y_c.reshape(bsz * s, sw), w_out_1.astype(bf), 0, d, jnp.float32)
    xf, _ = _memory_block(xf, y, ln1_g_1, ln1_b_1, memb, w_xq_1.astype(bf), w_xkv_1.astype(bf),
                          w_xo_1.astype(bf), ln2_g_1, ln2_b_1, bsz, s, False)
    return xf.reshape(bsz, s, d)
```

```python
import functools
import math

import numpy as np
import jax
import jax.numpy as jnp
from jax import lax
from jax.experimental import pallas as pl
from jax.experimental.pallas import tpu as pltpu

DEPTH = 2
HEAD_DIM = 128
CONV_K = 3
MOBA_BLOCK = 256
MOBA_TOPK = 3
XATTN_HEADS = 4
REL_BUCKETS = 32
REL_MAX_DIST = 128
LN_EPS = 1e-5
NEG_INF = -1e30
DEEPNORM_ALPHA = (2 * DEPTH) ** 0.25
LOG2E = math.log2(math.e)

MOBA_HEADS_PER_STEP = 2
SB_BLOCK = 256
SB_HEADS_PER_STEP = 4
SB_MAX_LOG2_ODDS = 60.0
SB_DEAD_BITS = 160.0
V7X_VMEM_BYTES = 64 * 1024 * 1024
VMEM_LIMIT = V7X_VMEM_BYTES - 8 * 1024 * 1024

_NT = (((1,), (1,)), ((), ()))


def _cparams(n_axes):
    return pltpu.CompilerParams(dimension_semantics=("arbitrary",) * n_axes,
                                vmem_limit_bytes=VMEM_LIMIT)


def _silu(g):
    return g * jax.nn.sigmoid(g)


def _split_bf16(x):
    hi = x.astype(jnp.bfloat16)
    lo = (x - hi.astype(jnp.float32)).astype(jnp.bfloat16)
    return hi, lo


def _mm_kernel(a_ref, w_ref, o_ref, *, scaled_blocks, scale):
    acc = jnp.dot(a_ref[...], w_ref[...], preferred_element_type=jnp.float32)
    if scaled_blocks:
        acc = acc * jnp.where(pl.program_id(1) < scaled_blocks, scale, 1.0)
    o_ref[...] = acc.astype(o_ref.dtype)


def _matmul(a, w, n_start, n_size, out_dtype, bm=1024, bn=1024, scaled_cols=0, scale=1.0):
    m, k = a.shape
    bm = min(bm, m)
    bn = math.gcd(math.gcd(math.gcd(bn, n_size), n_start), scaled_cols)
    assert m % bm == 0 and bn % 128 == 0
    joff = n_start // bn
    return pl.pallas_call(
        functools.partial(_mm_kernel, scaled_blocks=scaled_cols // bn, scale=scale),
        out_shape=jax.ShapeDtypeStruct((m, n_size), out_dtype),
        grid=(m // bm, n_size // bn),
        in_specs=[pl.BlockSpec((bm, k), lambda i, j: (i, 0)),
                  pl.BlockSpec((k, bn), lambda i, j: (0, j + joff))],
        out_specs=pl.BlockSpec((bm, bn), lambda i, j: (i, j)),
        compiler_params=_cparams(2),
        name="matmul",
    )(a, w)


def _mm2_kernel(a1_ref, a2_ref, w_ref, o_ref):
    k1 = a1_ref.shape[1]
    acc = jnp.dot(a1_ref[...], w_ref[0:k1, :], preferred_element_type=jnp.float32)
    acc = acc + jnp.dot(a2_ref[...], w_ref[k1:, :], preferred_element_type=jnp.float32)
    o_ref[...] = acc.astype(o_ref.dtype)


def _matmul_cat(a1, a2, w, out_dtype, bm=1024, bn=1024):
    m, k1 = a1.shape
    k2 = a2.shape[1]
    n = w.shape[1]
    bm = min(bm, m)
    bn = min(bn, n)
    assert m % bm == 0 and n % bn == 0 and w.shape[0] == k1 + k2
    return pl.pallas_call(
        _mm2_kernel,
        out_shape=jax.ShapeDtypeStruct((m, n), out_dtype),
        grid=(m // bm, n // bn),
        in_specs=[pl.BlockSpec((bm, k1), lambda i, j: (i, 0)),
                  pl.BlockSpec((bm, k2), lambda i, j: (i, 0)),
                  pl.BlockSpec((k1 + k2, bn), lambda i, j: (0, j))],
        out_specs=pl.BlockSpec((bm, bn), lambda i, j: (i, j)),
        compiler_params=_cparams(2),
        name="matmul_cat",
    )(a1, a2, w)


def _layer_norm_rows(r, g, b):
    mu = jnp.mean(r, axis=-1, keepdims=True)
    d = r - mu
    var = jnp.mean(d * d, axis=-1, keepdims=True)
    return d * lax.rsqrt(var + LN_EPS) * g + b


def _mm_ln_pipelined_kernel(a_ref, w_ref, x_ref, g_ref, b_ref, of_ref, *rest):
    ob_ref = rest[0] if len(rest) == 3 else None
    y_refs = rest[-2:]
    i = pl.program_id(0)

    def multiply(y_ref):
        y_ref[...] = jnp.dot(a_ref[...], w_ref[...], preferred_element_type=jnp.float32)

    def normalise(y_ref):
        o = _layer_norm_rows(DEEPNORM_ALPHA * x_ref[...] + y_ref[...], g_ref[...], b_ref[...])
        of_ref[...] = o
        if ob_ref is not None:
            ob_ref[...] = o.astype(jnp.bfloat16)

    @pl.when(i == 0)
    def _():
        multiply(y_refs[0])

    for parity in (0, 1):
        @pl.when((i > 0) & (i % 2 == parity))
        def _():
            multiply(y_refs[parity])
            normalise(y_refs[1 - parity])


def _matmul_residual_ln(a, w, x, g, b, emit_bf16, bm=256):
    m, k = a.shape
    d = w.shape[1]
    bm = min(bm, m)
    n_tiles = m // bm
    lagged = pl.BlockSpec((bm, d), lambda i: (jnp.maximum(i - 1, 0), 0))
    vec = pl.BlockSpec((1, d), lambda i: (0, 0))
    out_shape = [jax.ShapeDtypeStruct((m, d), jnp.float32)]
    if emit_bf16:
        out_shape.append(jax.ShapeDtypeStruct((m, d), jnp.bfloat16))
    out = pl.pallas_call(
        _mm_ln_pipelined_kernel,
        out_shape=tuple(out_shape),
        grid=(n_tiles + 1,),
        in_specs=[pl.BlockSpec((bm, k), lambda i: (jnp.minimum(i, n_tiles - 1), 0)),
                  pl.BlockSpec((k, d), lambda i: (0, 0)), lagged, vec, vec],
        out_specs=tuple([lagged] * len(out_shape)),
        scratch_shapes=[pltpu.VMEM((bm, d), jnp.float32)] * 2,
        compiler_params=_cparams(1),
        name="matmul_residual_ln",
    )(a, w, x, g.reshape(1, d), b.reshape(1, d))
    return out if emit_bf16 else (out[0], None)


def _conv_proj_kernel(a_ref, wh_ref, wb_ref, wc_ref, wg_ref, cw_ref, cast_ref, o_ref, cast_out_ref,
                      tail_ref, *, tiles_per_seq):
    i = pl.program_id(0)
    j = pl.program_id(1)
    cast_out_ref[...] = cast_ref[...].astype(cast_out_ref.dtype)
    a = a_ref[...]
    u = (jnp.dot(a, wc_ref[...], preferred_element_type=jnp.float32)
         * jnp.dot(a, wh_ref[...], preferred_element_type=jnp.float32))
    bm = u.shape[0]

    @pl.when(i % tiles_per_seq == 0)
    def _():
        tail_ref[j] = jnp.zeros(tail_ref.shape[1:], jnp.float32)

    prev = tail_ref[j]
    tail_ref[j] = u[bm - 8:bm, :]
    row = lax.broadcasted_iota(jnp.int32, u.shape, 0)
    u1 = jnp.where(row == 0, prev[7:8, :], pltpu.roll(u, 1, 0))
    u2 = jnp.where(row == 0, prev[6:7, :],
                   jnp.where(row == 1, prev[7:8, :], pltpu.roll(u, 2, 0)))
    w = cw_ref[...]
    conv = w[0:1, :] * u2 + w[1:2, :] * u1 + w[2:3, :] * u
    b = jnp.dot(a, wb_ref[...], preferred_element_type=jnp.float32)
    g = jnp.dot(a, wg_ref[...], preferred_element_type=jnp.float32)
    o_ref[...] = (b * conv * _silu(g)).astype(o_ref.dtype)


def _conv_proj(a, w_in, conv_w, width, seq_len, cast_src, bm=1024, bc=256):
    m, k = a.shape
    bm = min(bm, seq_len)
    bc = min(bc, width)
    nc = width // bc
    steps = (m // bm) * nc
    cast_rows, cast_cols = cast_src.shape
    slab = cast_rows // steps
    assert seq_len % bm == 0 and m % seq_len == 0 and width % bc == 0 and bm % 8 == 0
    assert cast_rows % steps == 0 and slab % 16 == 0

    def wcol(group):
        return pl.BlockSpec((k, bc), lambda i, j: (0, j + group * nc))

    cast_spec = pl.BlockSpec((slab, cast_cols), lambda i, j: (i * nc + j, 0))
    return pl.pallas_call(
        functools.partial(_conv_proj_kernel, tiles_per_seq=seq_len // bm),
        out_shape=(jax.ShapeDtypeStruct((m, width), jnp.bfloat16),
                   jax.ShapeDtypeStruct(cast_src.shape, jnp.bfloat16)),
        grid=(m // bm, nc),
        in_specs=[pl.BlockSpec((bm, k), lambda i, j: (i, 0)),
                  wcol(0), wcol(1), wcol(2), wcol(3),
                  pl.BlockSpec((CONV_K, bc), lambda i, j: (0, j)), cast_spec],
        out_specs=(pl.BlockSpec((bm, bc), lambda i, j: (i, j)), cast_spec),
        scratch_shapes=[pltpu.VMEM((nc, 8, bc), jnp.float32)],
        compiler_params=_cparams(2),
        name="conv_proj",
    )(a, w_in, w_in, w_in, w_in, conv_w, cast_src)


def _t5_bucket_thresholds():
    d = np.arange(0, 4 * REL_MAX_DIST)
    max_exact = REL_BUCKETS // 2
    nf = np.maximum(d, 1).astype(np.float32)
    large = max_exact + (np.log(nf / np.float32(max_exact)) / np.float32(math.log(REL_MAX_DIST / max_exact))
                         * np.float32(REL_BUCKETS - max_exact)).astype(np.int32)
    large = np.minimum(large, REL_BUCKETS - 1)
    bucket = np.where(d < max_exact, d, large)
    assert np.all(np.diff(bucket) >= 0) and bucket[-1] == REL_BUCKETS - 1
    return [int(np.argmax(bucket >= b)) for b in range(REL_BUCKETS)]


_BUCKET_THR = _t5_bucket_thresholds()
_MOBA_BLOCK_SHIFT = MOBA_BLOCK.bit_length() - 1
assert 1 << _MOBA_BLOCK_SHIFT == MOBA_BLOCK


def _moba_kernel(rb_ref, q_ref, k_ref, v_ref, g_ref, o_ref, t0_ref, t1_ref, kaug_ref, *, nb):
    blk = MOBA_BLOCK
    s_len = nb * blk
    heads = range(MOBA_HEADS_PER_STEP)
    head_ids = [pl.program_id(0) * MOBA_HEADS_PER_STEP + hh for hh in heads]
    cols = [slice(hh * HEAD_DIM, (hh + 1) * HEAD_DIM) for hh in heads]
    qo = lax.broadcasted_iota(jnp.int32, (blk, blk), 0)
    ko = lax.broadcasted_iota(jnp.int32, (blk, blk), 1)

    @pl.when(pl.program_id(1) == 0)
    def _():
        def table(h, dist):
            t = jnp.full(dist.shape, rb_ref[h, REL_BUCKETS - 1], jnp.float32)
            for bkt in range(REL_BUCKETS - 2, -1, -1):
                t = jnp.where(dist < _BUCKET_THR[bkt + 1], rb_ref[h, bkt], t)
            return t * LOG2E
        for hh in heads:
            t0_ref[hh] = table(head_ids[hh], qo - ko)
            t1_ref[hh] = table(head_ids[hh], qo - ko + blk)

    rb_far = [rb_ref[h, REL_BUCKETS - 1] * LOG2E for h in head_ids]
    assert _BUCKET_THR[REL_BUCKETS - 1] <= blk + 1

    block_onehot = jnp.where(
        jnp.right_shift(lax.broadcasted_iota(jnp.int32, (s_len, HEAD_DIM), 0), _MOBA_BLOCK_SHIFT)
        == lax.broadcasted_iota(jnp.int32, (s_len, HEAD_DIM), 1), 1.0, 0.0).astype(jnp.bfloat16)
    kbars = []
    for hh in heads:
        kaug_ref[hh, :, 0:HEAD_DIM] = k_ref[0, :, cols[hh]]
        kaug_ref[hh, :, HEAD_DIM:2 * HEAD_DIM] = block_onehot
        kbar = jnp.concatenate(
            [jnp.mean(k_ref[0, j * blk:(j + 1) * blk, cols[hh]].astype(jnp.float32),
                      axis=0, keepdims=True) for j in range(nb)]
            + [jnp.zeros((16 - nb, HEAD_DIM), jnp.float32)], axis=0)
        kbars.append(_split_bf16(kbar))

    eye = jnp.where(qo == ko, 1.0, 0.0).astype(jnp.bfloat16)
    jidx = lax.broadcasted_iota(jnp.int32, (16, blk), 0)

    def logits(hh, i):
        q_i = q_ref[0, i * blk:(i + 1) * blk, cols[hh]]
        kbar_hi, kbar_lo = kbars[hh]
        nk = (i + 1) * blk
        if i > MOBA_TOPK:
            gate = (lax.dot_general(kbar_hi, q_i, _NT, preferred_element_type=jnp.float32)
                    + lax.dot_general(kbar_lo, q_i, _NT, preferred_element_type=jnp.float32))
            cnt = jnp.zeros((16, blk), jnp.float32)
            for jp in range(i):
                row = gate[jp:jp + 1, :]
                beats = jnp.where(row > gate, 1.0,
                                  jnp.where(row == gate, jnp.where(jidx > jp, 1.0, 0.0), 0.0))
                cnt = cnt + beats
            drop_t = jnp.where(cnt < MOBA_TOPK, 0.0, jnp.where(jidx < i, NEG_INF, 0.0))
            drop_t = jnp.concatenate([drop_t, jnp.zeros((128 - 16, blk), jnp.float32)], axis=0)
            drop = lax.dot_general(eye, drop_t.astype(jnp.bfloat16), _NT,
                                   preferred_element_type=jnp.float32)
            q_aug = jnp.concatenate([q_i, drop.astype(jnp.bfloat16)], axis=-1)
            return lax.dot_general(q_aug, kaug_ref[hh, 0:nk, :], _NT,
                                   preferred_element_type=jnp.float32)
        return lax.dot_general(q_i, k_ref[0, 0:nk, cols[hh]], _NT,
                               preferred_element_type=jnp.float32)

    def attend(hh, i, s):
        nk = (i + 1) * blk
        n_far = max(i - 1, 0) * blk
        lg_own = jnp.where(ko <= qo, s[:, i * blk:nk] + t0_ref[hh], NEG_INF)
        m = jnp.max(lg_own, axis=-1, keepdims=True)
        if i > 0:
            lg_prev = s[:, (i - 1) * blk:i * blk] + t1_ref[hh]
            m = jnp.maximum(m, jnp.max(lg_prev, axis=-1, keepdims=True))
        if n_far:
            m = jnp.maximum(m, jnp.max(s[:, 0:n_far], axis=-1, keepdims=True) + rb_far[hh])
        pieces = [jnp.exp2(lg_own - m)]
        if i > 0:
            pieces.insert(0, jnp.exp2(lg_prev - m))
        if n_far:
            pieces.insert(0, jnp.exp2(s[:, 0:n_far] - (m - rb_far[hh])))
        l = sum(jnp.sum(piece, axis=-1, keepdims=True) for piece in pieces)
        p = pieces[0] if len(pieces) == 1 else jnp.concatenate(pieces, axis=-1)
        o = jnp.dot(p.astype(jnp.bfloat16), v_ref[0, 0:nk, cols[hh]],
                    preferred_element_type=jnp.float32) / l
        rows = slice(i * blk, (i + 1) * blk)
        o_ref[0, rows, cols[hh]] = (o * _silu(g_ref[0, rows, cols[hh]])).astype(o_ref.dtype)

    items = [(hh, i) for hh in heads for i in range(nb)]
    s_next = logits(*items[0])
    for n, item in enumerate(items):
        s_cur, s_next = s_next, (logits(*items[n + 1]) if n + 1 < len(items) else None)
        attend(*item, s_cur)


def _moba(qkv, g, rel_bias_t, heads):
    bsz, s, _ = qkv.shape
    assert s % MOBA_BLOCK == 0
    nb = s // MOBA_BLOCK
    assert MOBA_TOPK < nb <= 16 and heads % MOBA_HEADS_PER_STEP == 0
    steps = heads // MOBA_HEADS_PER_STEP
    width = MOBA_HEADS_PER_STEP * HEAD_DIM

    def col(group):
        return pl.BlockSpec((1, s, width), lambda h, b: (b, 0, h + group * steps))

    return pl.pallas_call(
        functools.partial(_moba_kernel, nb=nb),
        out_shape=jax.ShapeDtypeStruct((bsz, s, heads * HEAD_DIM), jnp.bfloat16),
        grid=(steps, bsz),
        in_specs=[pl.BlockSpec(memory_space=pltpu.SMEM), col(0), col(1), col(2), col(0)],
        out_specs=col(0),
        scratch_shapes=[pltpu.VMEM((MOBA_HEADS_PER_STEP, MOBA_BLOCK, MOBA_BLOCK), jnp.float32),
                        pltpu.VMEM((MOBA_HEADS_PER_STEP, MOBA_BLOCK, MOBA_BLOCK), jnp.float32),
                        pltpu.VMEM((MOBA_HEADS_PER_STEP, s, 2 * HEAD_DIM), jnp.bfloat16)],
        compiler_params=_cparams(2),
        name="moba",
    )(rel_bias_t, qkv, qkv, qkv, g)


def _sb_kernel(q_ref, k_ref, v_ref, g_ref, o_ref, za_ref, zb_ref, acc_ref, car_ref, *, nq):
    t = SB_BLOCK
    tq = 2 * t
    tri = jnp.where(lax.broadcasted_iota(jnp.int32, (t, t), 0)
                    > lax.broadcasted_iota(jnp.int32, (t, t), 1), 1.0, 0.0).astype(jnp.bfloat16)

    def logits(qs, j0):
        return tuple(lax.dot_general(q, k_ref[0, pl.ds(j0, t), hh * HEAD_DIM:(hh + 1) * HEAD_DIM],
                                     _NT, preferred_element_type=jnp.float32)
                     for hh, q in enumerate(qs))

    def tiles(zs, j0, rows, diag):
        heads = range(len(zs))
        cols = [slice(hh * HEAD_DIM, (hh + 1) * HEAD_DIM) for hh in heads]
        if diag:
            strict = (lax.broadcasted_iota(jnp.int32, zs[0].shape, 1)
                      < lax.broadcasted_iota(jnp.int32, zs[0].shape, 0))
            zs = [jnp.where(strict, z, NEG_INF) for z in zs]
        zs = [jnp.minimum(z, SB_MAX_LOG2_ODDS) for z in zs]
        sps = [jnp.log(1.0 + jnp.exp2(z)) * LOG2E for z in zs]
        rems = [jnp.dot(sps[hh].astype(jnp.bfloat16), tri, preferred_element_type=jnp.float32)
                + car_ref[hh, rows, :] for hh in heads]
        ws = [jnp.exp2((zs[hh] - sps[hh]) - rems[hh]).astype(jnp.bfloat16) for hh in heads]
        for hh in heads:
            acc_ref[hh, rows, :] += jnp.dot(ws[hh], v_ref[0, pl.ds(j0, t), cols[hh]],
                                            preferred_element_type=jnp.float32)
            car_ref[hh, rows, :] = rems[hh][:, 0:1] + sps[hh][:, 0:1]

    def q_block(i, _):
        q0 = pl.multiple_of(i * tq, tq)
        q1 = pl.multiple_of(i * tq + t, t)
        heads = range(SB_HEADS_PER_STEP)
        cols = [slice(hh * HEAD_DIM, (hh + 1) * HEAD_DIM) for hh in heads]
        q_i = [q_ref[0, pl.ds(q0, tq), c] for c in cols]
        acc_ref[...] = jnp.zeros(acc_ref.shape, jnp.float32)
        car_ref[...] = jnp.zeros(car_ref.shape, jnp.float32)

        def past_j0(n):
            return pl.multiple_of(jnp.maximum(2 * i - 1 - n, 0) * t, t)

        def stage_logits(z_ref, n, rows):
            for hh, z in enumerate(logits([q[rows, :] for q in q_i], past_j0(n))):
                z_ref[hh, rows, :] = z

        zs_low = logits([q_ref[0, pl.ds(q1, t), c] for c in cols], q1)
        zs_diag = logits(q_i, q0)
        stage_logits(za_ref, 0, slice(0, tq))
        tiles(zs_low, q1, slice(t, tq), True)
        tiles(zs_diag, q0, slice(0, tq), True)

        def all_dead(rows):
            return (jnp.min(jnp.min(car_ref[:, rows, :], axis=0)) >= SB_DEAD_BITS).astype(jnp.int32)

        def walk(rows):
            def past_pair(carry):
                m = carry[0]
                stage_logits(zb_ref, 2 * m + 1, rows)
                tiles([za_ref[hh, rows, :] for hh in heads], past_j0(2 * m), rows, False)

                @pl.when(all_dead(rows) == 0)
                def _():
                    stage_logits(za_ref, 2 * m + 2, rows)
                    tiles([zb_ref[hh, rows, :] for hh in heads], past_j0(2 * m + 1), rows, False)

                return m + 1, all_dead(rows)

            lax.while_loop(lambda carry: (carry[0] < i) & (carry[1] == 0), past_pair,
                           (jnp.int32(0), jnp.int32(0)))

        lower_dead = all_dead(slice(t, tq))

        @pl.when(lower_dead == 1)
        def _():
            walk(slice(0, t))

        @pl.when(lower_dead == 0)
        def _():
            walk(slice(0, tq))
        acc = jnp.concatenate([acc_ref[hh] for hh in heads], axis=-1)
        o_ref[0, pl.ds(q0, tq), :] = (acc * _silu(g_ref[0, pl.ds(q0, tq), :])).astype(o_ref.dtype)
        return 0

    lax.fori_loop(0, nq, q_block, 0)


def _stick_breaking(qkv, g, heads):
    bsz, s, _ = qkv.shape
    assert s % (2 * SB_BLOCK) == 0 and heads % SB_HEADS_PER_STEP == 0
    steps = heads // SB_HEADS_PER_STEP
    width = SB_HEADS_PER_STEP * HEAD_DIM

    def col(group):
        return pl.BlockSpec((1, s, width), lambda h, b: (b, 0, h + group * steps))

    return pl.pallas_call(
        functools.partial(_sb_kernel, nq=s // (2 * SB_BLOCK)),
        out_shape=jax.ShapeDtypeStruct((bsz, s, heads * HEAD_DIM), jnp.bfloat16),
        grid=(steps, bsz),
        in_specs=[col(0), col(1), col(2), col(0)],
        out_specs=col(0),
        scratch_shapes=[pltpu.VMEM((SB_HEADS_PER_STEP, 2 * SB_BLOCK, SB_BLOCK), jnp.float32)] * 2
        + [pltpu.VMEM((SB_HEADS_PER_STEP, 2 * SB_BLOCK, HEAD_DIM), jnp.float32),
           pltpu.VMEM((SB_HEADS_PER_STEP, 2 * SB_BLOCK, 1), jnp.float32)],
        compiler_params=_cparams(2),
        name="stick_breaking",
    )(qkv, qkv, qkv, g)


def _xattn_kernel(q_ref, g_ref, k_ref, v_ref, o_ref, *, dh):
    scale = dh ** -0.5
    for h in range(XATTN_HEADS):
        cols = slice(h * dh, (h + 1) * dh)
        s = lax.dot_general(q_ref[0, :, cols], k_ref[0, :, cols], _NT,
                            preferred_element_type=jnp.float32) * scale
        m = jnp.max(s, axis=-1, keepdims=True)
        p = jnp.exp(s - m)
        l = jnp.sum(p, axis=-1, keepdims=True)
        o = jnp.dot(p.astype(jnp.bfloat16), v_ref[0, :, cols],
                    preferred_element_type=jnp.float32) / l
        o_ref[0, :, cols] = (o * _silu(g_ref[0, :, cols])).astype(o_ref.dtype)


def _xattn(q, g, kv, bm=512):
    bsz, s, w = q.shape
    n_mem = kv.shape[1]
    bm = min(bm, s)
    row = pl.BlockSpec((1, bm, w), lambda b, i: (b, i, 0))
    return pl.pallas_call(
        functools.partial(_xattn_kernel, dh=w // XATTN_HEADS),
        out_shape=jax.ShapeDtypeStruct((bsz, s, w), jnp.bfloat16),
        grid=(bsz, s // bm),
        in_specs=[row, row,
                  pl.BlockSpec((1, n_mem, w), lambda b, i: (b, 0, 0)),
                  pl.BlockSpec((1, n_mem, w), lambda b, i: (b, 0, 1))],
        out_specs=row,
        compiler_params=_cparams(2),
        name="xattn",
    )(q, g, kv, kv)


def _ln_proj_kernel(x_ref, y_ref, g_ref, b_ref, w_ref, xf_ref, q_ref, gate_ref, a0_ref, a1_ref, *,
                    q_blocks):
    i = pl.program_id(0)
    j = pl.program_id(1)
    chunk = x_ref.shape[0]
    a_refs = (a0_ref, a1_ref)

    def normalise(a_ref):
        o = _layer_norm_rows(DEEPNORM_ALPHA * x_ref[...] + y_ref[...], g_ref[...], b_ref[...])
        xf_ref[...] = o
        a_ref[pl.ds(pl.multiple_of(j * chunk, chunk), chunk), :] = o.astype(jnp.bfloat16)

    def emit(acc):
        @pl.when(j < q_blocks)
        def _():
            q_ref[...] = acc.astype(q_ref.dtype)

        @pl.when(j >= q_blocks)
        def _():
            gate_ref[...] = acc

    @pl.when(i == 0)
    def _():
        normalise(a_refs[0])

    for parity in (0, 1):
        @pl.when((i > 0) & (i % 2 == parity))
        def _():
            acc = jnp.dot(a_refs[1 - parity][...], w_ref[...], preferred_element_type=jnp.float32)
            normalise(a_refs[parity])
            emit(acc)


def _residual_ln_proj(x, y, g, b, w, q_cols, bm=1024, bn=256):
    m, d = x.shape
    n = w.shape[1]
    bm = min(bm, m)
    bn = math.gcd(bn, q_cols, n - q_cols)
    nj = n // bn
    chunk = bm // nj
    n_tiles = m // bm
    q_blocks = q_cols // bn
    assert m % bm == 0 and bm % nj == 0 and chunk % 16 == 0 and 0 < q_blocks < nj

    def chunk_idx(i, j):
        return (jnp.where(i < n_tiles, i * nj + j, n_tiles * nj - 1), 0)

    def out_idx(col):
        return lambda i, j: (jnp.maximum(i - 1, 0), jnp.where(i == 0, 0, col(j)))

    rows = pl.BlockSpec((chunk, d), chunk_idx)
    vec = pl.BlockSpec((1, d), lambda i, j: (0, 0))
    return pl.pallas_call(
        functools.partial(_ln_proj_kernel, q_blocks=q_blocks),
        out_shape=(jax.ShapeDtypeStruct((m, d), jnp.float32),
                   jax.ShapeDtypeStruct((m, q_cols), jnp.bfloat16),
                   jax.ShapeDtypeStruct((m, n - q_cols), jnp.float32)),
        grid=(n_tiles + 1, nj),
        in_specs=[rows, rows, vec, vec, pl.BlockSpec((d, bn), lambda i, j: (0, j))],
        out_specs=(rows,
                   pl.BlockSpec((bm, bn), out_idx(lambda j: jnp.minimum(j, q_blocks - 1))),
                   pl.BlockSpec((bm, bn), out_idx(lambda j: jnp.maximum(j - q_blocks, 0)))),
        scratch_shapes=[pltpu.VMEM((bm, d), jnp.bfloat16)] * 2,
        compiler_params=_cparams(2),
        name="residual_ln_proj",
    )(x, y, g.reshape(1, d), b.reshape(1, d), w)


def _memory_block(x_f32, y, g1, b1, mem_bf16, w_xq, w_xkv, w_xo, g2, b2, bsz, s, emit_bf16):
    xw = w_xo.shape[0]
    x_f32, q, g = _residual_ln_proj(x_f32, y, g1, b1, w_xq, xw)
    kv = _matmul(mem_bf16, w_xkv, 0, 2 * xw, jnp.bfloat16)
    n_mem = mem_bf16.shape[0] // bsz
    o = _xattn(q.reshape(bsz, s, xw), g.reshape(bsz, s, xw), kv.reshape(bsz, n_mem, 2 * xw))
    return _matmul_residual_ln(o.reshape(bsz * s, xw), w_xo, x_f32, g2, b2, emit_bf16)


def kernel(x, mem, rel_bias, w_in_0, conv_w_0, w_out_0, ln1_g_0, ln1_b_0, w_xq_0, w_xkv_0, w_xo_0, ln2_g_0, ln2_b_0, w_in_1, w_out_1, ln1_g_1, ln1_b_1, w_xq_1, w_xkv_1, w_xo_1, ln2_g_1, ln2_b_1):
    bsz, s, d = x.shape
    bf = jnp.bfloat16
    xf = x.reshape(bsz * s, d)
    xb = xf.astype(bf)
    memb = mem.reshape(-1, d).astype(bf)

    cw = conv_w_0.shape[1]
    mw = w_in_0.shape[1] // 4 - cw
    heads = mw // HEAD_DIM
    w_in = w_in_0.astype(bf)
    y_a, w_in_1_bf16 = _conv_proj(xb, w_in, conv_w_0, cw, s, w_in_1)
    qkv = _matmul(xb, w_in, 4 * cw, 3 * mw, bf, scaled_cols=mw, scale=HEAD_DIM ** -0.5 * LOG2E)
    g_b = _matmul(xb, w_in, 4 * cw + 3 * mw, mw, jnp.float32)
    y_b = _moba(qkv.reshape(bsz, s, 3 * mw), g_b.reshape(bsz, s, mw), rel_bias.T, heads)
    y = _matmul_cat(y_a, y_b.reshape(bsz * s, mw), w_out_0.astype(bf), jnp.float32)
    xf, xb = _memory_block(xf, y, ln1_g_0, ln1_b_0, memb, w_xq_0.astype(bf), w_xkv_0.astype(bf),
                           w_xo_0.astype(bf), ln2_g_0, ln2_b_0, bsz, s, True)

    sw = w_out_1.shape[0]
    w_in = w_in_1_bf16
    qkv = _matmul(xb, w_in, 0, 3 * sw, bf, scaled_cols=sw, scale=HEAD_DIM ** -0.5 * LOG2E)
    g_c = _matmul(xb, w_in, 3 * sw, sw, jnp.float32)
    y_c = _stick_breaking(qkv.reshape(bsz, s, 3 * sw), g_c.reshape(bsz, s, sw), sw // HEAD_DIM)
    y = _matmul(y_c.reshape(bsz * s, sw), w_out_1.astype(bf), 0, d, jnp.float32)
    xf, _ = _memory_block(xf, y, ln1_g_1, ln1_b_1, memb, w_xq_1.astype(bf), w_xkv_1.astype(bf),
                          w_xo_1.astype(bf), ln2_g_1, ln2_b_1, bsz, s, False)
    return xf.reshape(bsz, s, d)
```

```python
import functools
import math

import numpy as np
import jax
import jax.numpy as jnp
from jax import lax
from jax.experimental import pallas as pl
from jax.experimental.pallas import tpu as pltpu

DEPTH = 2
HEAD_DIM = 128
CONV_K = 3
MOBA_BLOCK = 256
MOBA_TOPK = 3
XATTN_HEADS = 4
REL_BUCKETS = 32
REL_MAX_DIST = 128
LN_EPS = 1e-5
NEG_INF = -1e30
DEEPNORM_ALPHA = (2 * DEPTH) ** 0.25
LOG2E = math.log2(math.e)

MOBA_HEADS_PER_STEP = 2
SB_BLOCK = 256
SB_HEADS_PER_STEP = 4
SB_MAX_LOG2_ODDS = 60.0
SB_DEAD_BITS = 160.0
V7X_VMEM_BYTES = 64 * 1024 * 1024
VMEM_LIMIT = V7X_VMEM_BYTES - 8 * 1024 * 1024

_NT = (((1,), (1,)), ((), ()))


def _cparams(n_axes):
    return pltpu.CompilerParams(dimension_semantics=("arbitrary",) * n_axes,
                                vmem_limit_bytes=VMEM_LIMIT)


def _silu(g):
    return g * jax.nn.sigmoid(g)


def _split_bf16(x):
    hi = x.astype(jnp.bfloat16)
    lo = (x - hi.astype(jnp.float32)).astype(jnp.bfloat16)
    return hi, lo


def _mm_kernel(a_ref, w_ref, o_ref, *, scaled_blocks, scale):
    acc = jnp.dot(a_ref[...], w_ref[...], preferred_element_type=jnp.float32)
    if scaled_blocks:
        acc = acc * jnp.where(pl.program_id(1) < scaled_blocks, scale, 1.0)
    o_ref[...] = acc.astype(o_ref.dtype)


def _matmul(a, w, n_start, n_size, out_dtype, bm=1024, bn=1024, scaled_cols=0, scale=1.0):
    m, k = a.shape
    bm = min(bm, m)
    bn = math.gcd(math.gcd(math.gcd(bn, n_size), n_start), scaled_cols)
    assert m % bm == 0 and bn % 128 == 0
    joff = n_start // bn
    return pl.pallas_call(
        functools.partial(_mm_kernel, scaled_blocks=scaled_cols // bn, scale=scale),
        out_shape=jax.ShapeDtypeStruct((m, n_size), out_dtype),
        grid=(m // bm, n_size // bn),
        in_specs=[pl.BlockSpec((bm, k), lambda i, j: (i, 0)),
                  pl.BlockSpec((k, bn), lambda i, j: (0, j + joff))],
        out_specs=pl.BlockSpec((bm, bn), lambda i, j: (i, j)),
        compiler_params=_cparams(2),
        name="matmul",
    )(a, w)


def _mm2_kernel(a1_ref, a2_ref, w_ref, o_ref):
    k1 = a1_ref.shape[1]
    acc = jnp.dot(a1_ref[...], w_ref[0:k1, :], preferred_element_type=jnp.float32)
    acc = acc + jnp.dot(a2_ref[...], w_ref[k1:, :], preferred_element_type=jnp.float32)
    o_ref[...] = acc.astype(o_ref.dtype)


def _matmul_cat(a1, a2, w, out_dtype, bm=1024, bn=1024):
    m, k1 = a1.shape
    k2 = a2.shape[1]
    n = w.shape[1]
    bm = min(bm, m)
    bn = min(bn, n)
    assert m % bm == 0 and n % bn == 0 and w.shape[0] == k1 + k2
    return pl.pallas_call(
        _mm2_kernel,
        out_shape=jax.ShapeDtypeStruct((m, n), out_dtype),
        grid=(m // bm, n // bn),
        in_specs=[pl.BlockSpec((bm, k1), lambda i, j: (i, 0)),
                  pl.BlockSpec((bm, k2), lambda i, j: (i, 0)),
                  pl.BlockSpec((k1 + k2, bn), lambda i, j: (0, j))],
        out_specs=pl.BlockSpec((bm, bn), lambda i, j: (i, j)),
        compiler_params=_cparams(2),
        name="matmul_cat",
    )(a1, a2, w)


def _layer_norm_rows(r, g, b):
    mu = jnp.mean(r, axis=-1, keepdims=True)
    d = r - mu
    var = jnp.mean(d * d, axis=-1, keepdims=True)
    return d * lax.rsqrt(var + LN_EPS) * g + b


def _mm_ln_pipelined_kernel(a_ref, w_ref, x_ref, g_ref, b_ref, of_ref, *rest):
    ob_ref = rest[0] if len(rest) == 3 else None
    y_refs = rest[-2:]
    i = pl.program_id(0)

    def multiply(y_ref):
        y_ref[...] = jnp.dot(a_ref[...], w_ref[...], preferred_element_type=jnp.float32)

    def normalise(y_ref):
        o = _layer_norm_rows(DEEPNORM_ALPHA * x_ref[...] + y_ref[...], g_ref[...], b_ref[...])
        of_ref[...] = o
        if ob_ref is not None:
            ob_ref[...] = o.astype(jnp.bfloat16)

    @pl.when(i == 0)
    def _():
        multiply(y_refs[0])

    for parity in (0, 1):
        @pl.when((i > 0) & (i % 2 == parity))
        def _():
            multiply(y_refs[parity])
            normalise(y_refs[1 - parity])


def _matmul_residual_ln(a, w, x, g, b, emit_bf16, bm=256):
    m, k = a.shape
    d = w.shape[1]
    bm = min(bm, m)
    n_tiles = m // bm
    lagged = pl.BlockSpec((bm, d), lambda i: (jnp.maximum(i - 1, 0), 0))
    vec = pl.BlockSpec((1, d), lambda i: (0, 0))
    out_shape = [jax.ShapeDtypeStruct((m, d), jnp.float32)]
    if emit_bf16:
        out_shape.append(jax.ShapeDtypeStruct((m, d), jnp.bfloat16))
    out = pl.pallas_call(
        _mm_ln_pipelined_kernel,
        out_shape=tuple(out_shape),
        grid=(n_tiles + 1,),
        in_specs=[pl.BlockSpec((bm, k), lambda i: (jnp.minimum(i, n_tiles - 1), 0)),
                  pl.BlockSpec((k, d), lambda i: (0, 0)), lagged, vec, vec],
        out_specs=tuple([lagged] * len(out_shape)),
        scratch_shapes=[pltpu.VMEM((bm, d), jnp.float32)] * 2,
        compiler_params=_cparams(1),
        name="matmul_residual_ln",
    )(a, w, x, g.reshape(1, d), b.reshape(1, d))
    return out if emit_bf16 else (out[0], None)


def _conv_proj_kernel(a_ref, wh_ref, wb_ref, wc_ref, wg_ref, cw_ref, *rest, tiles_per_seq, n_cast):
    cast_refs, o_ref = rest[:n_cast], rest[n_cast]
    cast_out_refs, tail_ref = rest[n_cast + 1:2 * n_cast + 1], rest[2 * n_cast + 1]
    i = pl.program_id(0)
    j = pl.program_id(1)
    for src_ref, dst_ref in zip(cast_refs, cast_out_refs):
        dst_ref[...] = src_ref[...].astype(dst_ref.dtype)
    a = a_ref[...]
    u = (jnp.dot(a, wc_ref[...], preferred_element_type=jnp.float32)
         * jnp.dot(a, wh_ref[...], preferred_element_type=jnp.float32))
    bm = u.shape[0]

    @pl.when(i % tiles_per_seq == 0)
    def _():
        tail_ref[j] = jnp.zeros(tail_ref.shape[1:], jnp.float32)

    prev = tail_ref[j]
    tail_ref[j] = u[bm - 8:bm, :]
    row = lax.broadcasted_iota(jnp.int32, u.shape, 0)
    u1 = jnp.where(row == 0, prev[7:8, :], pltpu.roll(u, 1, 0))
    u2 = jnp.where(row == 0, prev[6:7, :],
                   jnp.where(row == 1, prev[7:8, :], pltpu.roll(u, 2, 0)))
    w = cw_ref[...]
    conv = w[0:1, :] * u2 + w[1:2, :] * u1 + w[2:3, :] * u
    b = jnp.dot(a, wb_ref[...], preferred_element_type=jnp.float32)
    g = jnp.dot(a, wg_ref[...], preferred_element_type=jnp.float32)
    o_ref[...] = (b * conv * _silu(g)).astype(o_ref.dtype)


def _conv_proj(a, w_in, conv_w, width, seq_len, cast_srcs, bm=1024, bc=256):
    m, k = a.shape
    bm = min(bm, seq_len)
    bc = min(bc, width)
    nc = width // bc
    steps = (m // bm) * nc
    assert seq_len % bm == 0 and m % seq_len == 0 and width % bc == 0 and bm % 8 == 0

    def wcol(group):
        return pl.BlockSpec((k, bc), lambda i, j: (0, j + group * nc))

    bf16_rows = 16
    flat = []
    for src in cast_srcs:
        rows = src.shape[0] if src.shape[0] % (steps * bf16_rows) == 0 else steps * bf16_rows
        assert src.size % (rows * 128) == 0
        flat.append(src.reshape(rows, src.size // rows))
    cast_specs = [pl.BlockSpec((f.shape[0] // steps, f.shape[1]), lambda i, j: (i * nc + j, 0))
                  for f in flat]
    out = pl.pallas_call(
        functools.partial(_conv_proj_kernel, tiles_per_seq=seq_len // bm, n_cast=len(flat)),
        out_shape=(jax.ShapeDtypeStruct((m, width), jnp.bfloat16),
                   *[jax.ShapeDtypeStruct(f.shape, jnp.bfloat16) for f in flat]),
        grid=(m // bm, nc),
        in_specs=[pl.BlockSpec((bm, k), lambda i, j: (i, 0)),
                  wcol(0), wcol(1), wcol(2), wcol(3),
                  pl.BlockSpec((CONV_K, bc), lambda i, j: (0, j)), *cast_specs],
        out_specs=(pl.BlockSpec((bm, bc), lambda i, j: (i, j)), *cast_specs),
        scratch_shapes=[pltpu.VMEM((nc, 8, bc), jnp.float32)],
        compiler_params=_cparams(2),
        name="conv_proj",
    )(a, w_in, w_in, w_in, w_in, conv_w, *flat)
    return out[0], [o.reshape(src.shape) for o, src in zip(out[1:], cast_srcs)]


def _t5_bucket_thresholds():
    d = np.arange(0, 4 * REL_MAX_DIST)
    max_exact = REL_BUCKETS // 2
    nf = np.maximum(d, 1).astype(np.float32)
    large = max_exact + (np.log(nf / np.float32(max_exact)) / np.float32(math.log(REL_MAX_DIST / max_exact))
                         * np.float32(REL_BUCKETS - max_exact)).astype(np.int32)
    large = np.minimum(large, REL_BUCKETS - 1)
    bucket = np.where(d < max_exact, d, large)
    assert np.all(np.diff(bucket) >= 0) and bucket[-1] == REL_BUCKETS - 1
    return [int(np.argmax(bucket >= b)) for b in range(REL_BUCKETS)]


_BUCKET_THR = _t5_bucket_thresholds()
_MOBA_BLOCK_SHIFT = MOBA_BLOCK.bit_length() - 1
assert 1 << _MOBA_BLOCK_SHIFT == MOBA_BLOCK


def _moba_kernel(rb_ref, q_ref, k_ref, v_ref, g_ref, o_ref, t0_ref, t1_ref, kaug_ref, *, nb):
    blk = MOBA_BLOCK
    s_len = nb * blk
    heads = range(MOBA_HEADS_PER_STEP)
    head_ids = [pl.program_id(0) * MOBA_HEADS_PER_STEP + hh for hh in heads]
    cols = [slice(hh * HEAD_DIM, (hh + 1) * HEAD_DIM) for hh in heads]
    qo = lax.broadcasted_iota(jnp.int32, (blk, blk), 0)
    ko = lax.broadcasted_iota(jnp.int32, (blk, blk), 1)

    @pl.when(pl.program_id(1) == 0)
    def _():
        def table(h, dist):
            t = jnp.full(dist.shape, rb_ref[h, REL_BUCKETS - 1], jnp.float32)
            for bkt in range(REL_BUCKETS - 2, -1, -1):
                t = jnp.where(dist < _BUCKET_THR[bkt + 1], rb_ref[h, bkt], t)
            return t * LOG2E
        for hh in heads:
            t0_ref[hh] = table(head_ids[hh], qo - ko)
            t1_ref[hh] = table(head_ids[hh], qo - ko + blk)

    rb_far = [rb_ref[h, REL_BUCKETS - 1] * LOG2E for h in head_ids]
    assert _BUCKET_THR[REL_BUCKETS - 1] <= blk + 1

    block_onehot = jnp.where(
        jnp.right_shift(lax.broadcasted_iota(jnp.int32, (s_len, HEAD_DIM), 0), _MOBA_BLOCK_SHIFT)
        == lax.broadcasted_iota(jnp.int32, (s_len, HEAD_DIM), 1), 1.0, 0.0).astype(jnp.bfloat16)
    kbars = []
    for hh in heads:
        kaug_ref[hh, :, 0:HEAD_DIM] = k_ref[0, :, cols[hh]]
        kaug_ref[hh, :, HEAD_DIM:2 * HEAD_DIM] = block_onehot
        kbar = jnp.concatenate(
            [jnp.mean(k_ref[0, j * blk:(j + 1) * blk, cols[hh]].astype(jnp.float32),
                      axis=0, keepdims=True) for j in range(nb)]
            + [jnp.zeros((16 - nb, HEAD_DIM), jnp.float32)], axis=0)
        kbars.append(_split_bf16(kbar))

    eye = jnp.where(qo == ko, 1.0, 0.0).astype(jnp.bfloat16)
    jidx = lax.broadcasted_iota(jnp.int32, (16, blk), 0)

    def logits(hh, i):
        q_i = q_ref[0, i * blk:(i + 1) * blk, cols[hh]]
        kbar_hi, kbar_lo = kbars[hh]
        nk = (i + 1) * blk
        if i > MOBA_TOPK:
            gate = (lax.dot_general(kbar_hi, q_i, _NT, preferred_element_type=jnp.float32)
                    + lax.dot_general(kbar_lo, q_i, _NT, preferred_element_type=jnp.float32))
            cnt = jnp.zeros((16, blk), jnp.float32)
            for jp in range(i):
                row = gate[jp:jp + 1, :]
                beats = jnp.where(row > gate, 1.0,
                                  jnp.where(row == gate, jnp.where(jidx > jp, 1.0, 0.0), 0.0))
                cnt = cnt + beats
            drop_t = jnp.where(cnt < MOBA_TOPK, 0.0, jnp.where(jidx < i, NEG_INF, 0.0))
            drop_t = jnp.concatenate([drop_t, jnp.zeros((128 - 16, blk), jnp.float32)], axis=0)
            drop = lax.dot_general(eye, drop_t.astype(jnp.bfloat16), _NT,
                                   preferred_element_type=jnp.float32)
            q_aug = jnp.concatenate([q_i, drop.astype(jnp.bfloat16)], axis=-1)
            return lax.dot_general(q_aug, kaug_ref[hh, 0:nk, :], _NT,
                                   preferred_element_type=jnp.float32)
        return lax.dot_general(q_i, k_ref[0, 0:nk, cols[hh]], _NT,
                               preferred_element_type=jnp.float32)

    def attend(hh, i, s):
        nk = (i + 1) * blk
        n_far = max(i - 1, 0) * blk
        lg_own = jnp.where(ko <= qo, s[:, i * blk:nk] + t0_ref[hh], NEG_INF)
        m = jnp.max(lg_own, axis=-1, keepdims=True)
        if i > 0:
            lg_prev = s[:, (i - 1) * blk:i * blk] + t1_ref[hh]
            m = jnp.maximum(m, jnp.max(lg_prev, axis=-1, keepdims=True))
        if n_far:
            m = jnp.maximum(m, jnp.max(s[:, 0:n_far], axis=-1, keepdims=True) + rb_far[hh])
        pieces = [jnp.exp2(lg_own - m)]
        if i > 0:
            pieces.insert(0, jnp.exp2(lg_prev - m))
        if n_far:
            pieces.insert(0, jnp.exp2(s[:, 0:n_far] - (m - rb_far[hh])))
        l = sum(jnp.sum(piece, axis=-1, keepdims=True) for piece in pieces)
        p = pieces[0] if len(pieces) == 1 else jnp.concatenate(pieces, axis=-1)
        o = jnp.dot(p.astype(jnp.bfloat16), v_ref[0, 0:nk, cols[hh]],
                    preferred_element_type=jnp.float32) / l
        rows = slice(i * blk, (i + 1) * blk)
        o_ref[0, rows, cols[hh]] = (o * _silu(g_ref[0, rows, cols[hh]])).astype(o_ref.dtype)

    items = [(hh, i) for hh in heads for i in range(nb)]
    s_next = logits(*items[0])
    for n, item in enumerate(items):
        s_cur, s_next = s_next, (logits(*items[n + 1]) if n + 1 < len(items) else None)
        attend(*item, s_cur)


def _moba(qkv, g, rel_bias_t, heads):
    bsz, s, _ = qkv.shape
    assert s % MOBA_BLOCK == 0
    nb = s // MOBA_BLOCK
    assert MOBA_TOPK < nb <= 16 and heads % MOBA_HEADS_PER_STEP == 0
    steps = heads // MOBA_HEADS_PER_STEP
    width = MOBA_HEADS_PER_STEP * HEAD_DIM

    def col(group):
        return pl.BlockSpec((1, s, width), lambda h, b: (b, 0, h + group * steps))

    return pl.pallas_call(
        functools.partial(_moba_kernel, nb=nb),
        out_shape=jax.ShapeDtypeStruct((bsz, s, heads * HEAD_DIM), jnp.bfloat16),
        grid=(steps, bsz),
        in_specs=[pl.BlockSpec(memory_space=pltpu.SMEM), col(0), col(1), col(2), col(0)],
        out_specs=col(0),
        scratch_shapes=[pltpu.VMEM((MOBA_HEADS_PER_STEP, MOBA_BLOCK, MOBA_BLOCK), jnp.float32),
                        pltpu.VMEM((MOBA_HEADS_PER_STEP, MOBA_BLOCK, MOBA_BLOCK), jnp.float32),
                        pltpu.VMEM((MOBA_HEADS_PER_STEP, s, 2 * HEAD_DIM), jnp.bfloat16)],
        compiler_params=_cparams(2),
        name="moba",
    )(rel_bias_t, qkv, qkv, qkv, g)


def _sb_kernel(q_ref, k_ref, v_ref, g_ref, o_ref, za_ref, zb_ref, acc_ref, car_ref, *, nq):
    t = SB_BLOCK
    tq = 2 * t
    tri = jnp.where(lax.broadcasted_iota(jnp.int32, (t, t), 0)
                    > lax.broadcasted_iota(jnp.int32, (t, t), 1), 1.0, 0.0).astype(jnp.bfloat16)

    def logits(qs, j0):
        return tuple(lax.dot_general(q, k_ref[0, pl.ds(j0, t), hh * HEAD_DIM:(hh + 1) * HEAD_DIM],
                                     _NT, preferred_element_type=jnp.float32)
                     for hh, q in enumerate(qs))

    def tiles(zs, j0, rows, diag):
        heads = range(len(zs))
        cols = [slice(hh * HEAD_DIM, (hh + 1) * HEAD_DIM) for hh in heads]
        if diag:
            strict = (lax.broadcasted_iota(jnp.int32, zs[0].shape, 1)
                      < lax.broadcasted_iota(jnp.int32, zs[0].shape, 0))
            zs = [jnp.where(strict, z, NEG_INF) for z in zs]
        zs = [jnp.minimum(z, SB_MAX_LOG2_ODDS) for z in zs]
        sps = [jnp.log(1.0 + jnp.exp2(z)) * LOG2E for z in zs]
        rems = [jnp.dot(sps[hh].astype(jnp.bfloat16), tri, preferred_element_type=jnp.float32)
                + car_ref[hh, rows, :] for hh in heads]
        ws = [jnp.exp2((zs[hh] - sps[hh]) - rems[hh]).astype(jnp.bfloat16) for hh in heads]
        for hh in heads:
            acc_ref[hh, rows, :] += jnp.dot(ws[hh], v_ref[0, pl.ds(j0, t), cols[hh]],
                                            preferred_element_type=jnp.float32)
            car_ref[hh, rows, :] = rems[hh][:, 0:1] + sps[hh][:, 0:1]

    def q_block(i, _):
        q0 = pl.multiple_of(i * tq, tq)
        q1 = pl.multiple_of(i * tq + t, t)
        heads = range(SB_HEADS_PER_STEP)
        cols = [slice(hh * HEAD_DIM, (hh + 1) * HEAD_DIM) for hh in heads]
        q_i = [q_ref[0, pl.ds(q0, tq), c] for c in cols]
        acc_ref[...] = jnp.zeros(acc_ref.shape, jnp.float32)
        car_ref[...] = jnp.zeros(car_ref.shape, jnp.float32)

        def past_j0(n):
            return pl.multiple_of(jnp.maximum(2 * i - 1 - n, 0) * t, t)

        def stage_logits(z_ref, n, rows):
            for hh, z in enumerate(logits([q[rows, :] for q in q_i], past_j0(n))):
                z_ref[hh, rows, :] = z

        zs_low = logits([q_ref[0, pl.ds(q1, t), c] for c in cols], q1)
        zs_diag = logits(q_i, q0)
        stage_logits(za_ref, 0, slice(0, tq))
        tiles(zs_low, q1, slice(t, tq), True)
        tiles(zs_diag, q0, slice(0, tq), True)

        def all_dead(rows):
            return (jnp.min(jnp.min(car_ref[:, rows, :], axis=0)) >= SB_DEAD_BITS).astype(jnp.int32)

        def walk(rows):
            def past_pair(carry):
                m = carry[0]
                stage_logits(zb_ref, 2 * m + 1, rows)
                tiles([za_ref[hh, rows, :] for hh in heads], past_j0(2 * m), rows, False)

                @pl.when(all_dead(rows) == 0)
                def _():
                    stage_logits(za_ref, 2 * m + 2, rows)
                    tiles([zb_ref[hh, rows, :] for hh in heads], past_j0(2 * m + 1), rows, False)

                return m + 1, all_dead(rows)

            lax.while_loop(lambda carry: (carry[0] < i) & (carry[1] == 0), past_pair,
                           (jnp.int32(0), jnp.int32(0)))

        lower_dead = all_dead(slice(t, tq))

        @pl.when(lower_dead == 1)
        def _():
            walk(slice(0, t))

        @pl.when(lower_dead == 0)
        def _():
            walk(slice(0, tq))
        acc = jnp.concatenate([acc_ref[hh] for hh in heads], axis=-1)
        o_ref[0, pl.ds(q0, tq), :] = (acc * _silu(g_ref[0, pl.ds(q0, tq), :])).astype(o_ref.dtype)
        return 0

    lax.fori_loop(0, nq, q_block, 0)


def _stick_breaking(qkv, g, heads):
    bsz, s, _ = qkv.shape
    assert s % (2 * SB_BLOCK) == 0 and heads % SB_HEADS_PER_STEP == 0
    steps = heads // SB_HEADS_PER_STEP
    width = SB_HEADS_PER_STEP * HEAD_DIM

    def col(group):
        return pl.BlockSpec((1, s, width), lambda h, b: (b, 0, h + group * steps))

    return pl.pallas_call(
        functools.partial(_sb_kernel, nq=s // (2 * SB_BLOCK)),
        out_shape=jax.ShapeDtypeStruct((bsz, s, heads * HEAD_DIM), jnp.bfloat16),
        grid=(steps, bsz),
        in_specs=[col(0), col(1), col(2), col(0)],
        out_specs=col(0),
        scratch_shapes=[pltpu.VMEM((SB_HEADS_PER_STEP, 2 * SB_BLOCK, SB_BLOCK), jnp.float32)] * 2
        + [pltpu.VMEM((SB_HEADS_PER_STEP, 2 * SB_BLOCK, HEAD_DIM), jnp.float32),
           pltpu.VMEM((SB_HEADS_PER_STEP, 2 * SB_BLOCK, 1), jnp.float32)],
        compiler_params=_cparams(2),
        name="stick_breaking",
    )(qkv, qkv, qkv, g)


def _xattn_kernel(q_ref, g_ref, k_ref, v_ref, o_ref, *, dh):
    scale = dh ** -0.5
    for h in range(XATTN_HEADS):
        cols = slice(h * dh, (h + 1) * dh)
        s = lax.dot_general(q_ref[0, :, cols], k_ref[0, :, cols], _NT,
                            preferred_element_type=jnp.float32) * scale
        m = jnp.max(s, axis=-1, keepdims=True)
        p = jnp.exp(s - m)
        l = jnp.sum(p, axis=-1, keepdims=True)
        o = jnp.dot(p.astype(jnp.bfloat16), v_ref[0, :, cols],
                    preferred_element_type=jnp.float32) / l
        o_ref[0, :, cols] = (o * _silu(g_ref[0, :, cols])).astype(o_ref.dtype)


def _xattn(q, g, kv, bm=512):
    bsz, s, w = q.shape
    n_mem = kv.shape[1]
    bm = min(bm, s)
    row = pl.BlockSpec((1, bm, w), lambda b, i: (b, i, 0))
    return pl.pallas_call(
        functools.partial(_xattn_kernel, dh=w // XATTN_HEADS),
        out_shape=jax.ShapeDtypeStruct((bsz, s, w), jnp.bfloat16),
        grid=(bsz, s // bm),
        in_specs=[row, row,
                  pl.BlockSpec((1, n_mem, w), lambda b, i: (b, 0, 0)),
                  pl.BlockSpec((1, n_mem, w), lambda b, i: (b, 0, 1))],
        out_specs=row,
        compiler_params=_cparams(2),
        name="xattn",
    )(q, g, kv, kv)


def _ln_proj_kernel(x_ref, y_ref, g_ref, b_ref, w_ref, xf_ref, q_ref, gate_ref, a0_ref, a1_ref, *,
                    q_blocks):
    i = pl.program_id(0)
    j = pl.program_id(1)
    chunk = x_ref.shape[0]
    a_refs = (a0_ref, a1_ref)

    def normalise(a_ref):
        o = _layer_norm_rows(DEEPNORM_ALPHA * x_ref[...] + y_ref[...], g_ref[...], b_ref[...])
        xf_ref[...] = o
        a_ref[pl.ds(pl.multiple_of(j * chunk, chunk), chunk), :] = o.astype(jnp.bfloat16)

    def emit(acc):
        @pl.when(j < q_blocks)
        def _():
            q_ref[...] = acc.astype(q_ref.dtype)

        @pl.when(j >= q_blocks)
        def _():
            gate_ref[...] = acc

    @pl.when(i == 0)
    def _():
        normalise(a_refs[0])

    for parity in (0, 1):
        @pl.when((i > 0) & (i % 2 == parity))
        def _():
            acc = jnp.dot(a_refs[1 - parity][...], w_ref[...], preferred_element_type=jnp.float32)
            normalise(a_refs[parity])
            emit(acc)


def _residual_ln_proj(x, y, g, b, w, q_cols, bm=1024, bn=256):
    m, d = x.shape
    n = w.shape[1]
    bm = min(bm, m)
    bn = math.gcd(bn, q_cols, n - q_cols)
    nj = n // bn
    chunk = bm // nj
    n_tiles = m // bm
    q_blocks = q_cols // bn
    assert m % bm == 0 and bm % nj == 0 and chunk % 16 == 0 and 0 < q_blocks < nj

    def chunk_idx(i, j):
        return (jnp.where(i < n_tiles, i * nj + j, n_tiles * nj - 1), 0)

    def out_idx(col):
        return lambda i, j: (jnp.maximum(i - 1, 0), jnp.where(i == 0, 0, col(j)))

    rows = pl.BlockSpec((chunk, d), chunk_idx)
    vec = pl.BlockSpec((1, d), lambda i, j: (0, 0))
    return pl.pallas_call(
        functools.partial(_ln_proj_kernel, q_blocks=q_blocks),
        out_shape=(jax.ShapeDtypeStruct((m, d), jnp.float32),
                   jax.ShapeDtypeStruct((m, q_cols), jnp.bfloat16),
                   jax.ShapeDtypeStruct((m, n - q_cols), jnp.float32)),
        grid=(n_tiles + 1, nj),
        in_specs=[rows, rows, vec, vec, pl.BlockSpec((d, bn), lambda i, j: (0, j))],
        out_specs=(rows,
                   pl.BlockSpec((bm, bn), out_idx(lambda j: jnp.minimum(j, q_blocks - 1))),
                   pl.BlockSpec((bm, bn), out_idx(lambda j: jnp.maximum(j - q_blocks, 0)))),
        scratch_shapes=[pltpu.VMEM((bm, d), jnp.bfloat16)] * 2,
        compiler_params=_cparams(2),
        name="residual_ln_proj",
    )(x, y, g.reshape(1, d), b.reshape(1, d), w)


def _memory_block(x_f32, y, g1, b1, mem_bf16, w_xq, w_xkv, w_xo, g2, b2, bsz, s, emit_bf16):
    xw = w_xo.shape[0]
    x_f32, q, g = _residual_ln_proj(x_f32, y, g1, b1, w_xq, xw)
    kv = _matmul(mem_bf16, w_xkv, 0, 2 * xw, jnp.bfloat16)
    n_mem = mem_bf16.shape[0] // bsz
    o = _xattn(q.reshape(bsz, s, xw), g.reshape(bsz, s, xw), kv.reshape(bsz, n_mem, 2 * xw))
    return _matmul_residual_ln(o.reshape(bsz * s, xw), w_xo, x_f32, g2, b2, emit_bf16)


def kernel(x, mem, rel_bias, w_in_0, conv_w_0, w_out_0, ln1_g_0, ln1_b_0, w_xq_0, w_xkv_0, w_xo_0, ln2_g_0, ln2_b_0, w_in_1, w_out_1, ln1_g_1, ln1_b_1, w_xq_1, w_xkv_1, w_xo_1, ln2_g_1, ln2_b_1):
    bsz, s, d = x.shape
    bf = jnp.bfloat16
    xf = x.reshape(bsz * s, d)
    xb = xf.astype(bf)
    memb = mem.reshape(-1, d).astype(bf)

    cw = conv_w_0.shape[1]
    mw = w_in_0.shape[1] // 4 - cw
    heads = mw // HEAD_DIM
    w_in = w_in_0.astype(bf)
    later = [w_out_0, w_xq_0, w_xkv_0, w_xo_0, w_in_1, w_out_1, w_xq_1, w_xkv_1, w_xo_1]
    y_a, later = _conv_proj(xb, w_in, conv_w_0, cw, s, later)
    w_out_0, w_xq_0, w_xkv_0, w_xo_0, w_in_1, w_out_1, w_xq_1, w_xkv_1, w_xo_1 = later
    qkv = _matmul(xb, w_in, 4 * cw, 3 * mw, bf, scaled_cols=mw, scale=HEAD_DIM ** -0.5 * LOG2E)
    g_b = _matmul(xb, w_in, 4 * cw + 3 * mw, mw, jnp.float32)
    y_b = _moba(qkv.reshape(bsz, s, 3 * mw), g_b.reshape(bsz, s, mw), rel_bias.T, heads)
    y = _matmul_cat(y_a, y_b.reshape(bsz * s, mw), w_out_0, jnp.float32)
    xf, xb = _memory_block(xf, y, ln1_g_0, ln1_b_0, memb, w_xq_0, w_xkv_0, w_xo_0,
                           ln2_g_0, ln2_b_0, bsz, s, True)

    sw = w_out_1.shape[0]
    qkv = _matmul(xb, w_in_1, 0, 3 * sw, bf, scaled_cols=sw, scale=HEAD_DIM ** -0.5 * LOG2E)
    g_c = _matmul(xb, w_in_1, 3 * sw, sw, jnp.float32)
    y_c = _stick_breaking(qkv.reshape(bsz, s, 3 * sw), g_c.reshape(bsz, s, sw), sw // HEAD_DIM)
    y = _matmul(y_c.reshape(bsz * s, sw), w_out_1, 0, d, jnp.float32)
    xf, _ = _memory_block(xf, y, ln1_g_1, ln1_b_1, memb, w_xq_1, w_xkv_1, w_xo_1,
                          ln2_g_1, ln2_b_1, bsz, s, False)
    return xf.reshape(bsz, s, d)
```

```python
import functools
import math

import numpy as np
import jax
import jax.numpy as jnp
from jax import lax
from jax.experimental import pallas as pl
from jax.experimental.pallas import tpu as pltpu

DEPTH = 2
HEAD_DIM = 128
CONV_K = 3
MOBA_BLOCK = 256
MOBA_TOPK = 3
XATTN_HEADS = 4
REL_BUCKETS = 32
REL_MAX_DIST = 128
LN_EPS = 1e-5
NEG_INF = -1e30
DEEPNORM_ALPHA = (2 * DEPTH) ** 0.25
LOG2E = math.log2(math.e)

MOBA_HEADS_PER_STEP = 2
SB_BLOCK = 256
SB_HEADS_PER_STEP = 4
SB_MAX_LOG2_ODDS = 60.0
SB_DEAD_BITS = 160.0
V7X_VMEM_BYTES = 64 * 1024 * 1024
VMEM_LIMIT = V7X_VMEM_BYTES - 8 * 1024 * 1024

_NT = (((1,), (1,)), ((), ()))


def _cparams(n_axes):
    return pltpu.CompilerParams(dimension_semantics=("arbitrary",) * n_axes,
                                vmem_limit_bytes=VMEM_LIMIT)


def _silu(g):
    return g * jax.nn.sigmoid(g)


def _split_bf16(x):
    hi = x.astype(jnp.bfloat16)
    lo = (x - hi.astype(jnp.float32)).astype(jnp.bfloat16)
    return hi, lo


def _mm_kernel(a_ref, w_ref, o_ref, *, scaled_blocks, scale):
    acc = jnp.dot(a_ref[...], w_ref[...], preferred_element_type=jnp.float32)
    if scaled_blocks:
        acc = acc * jnp.where(pl.program_id(1) < scaled_blocks, scale, 1.0)
    o_ref[...] = acc.astype(o_ref.dtype)


def _matmul(a, w, n_start, n_size, out_dtype, bm=1024, bn=1024, scaled_cols=0, scale=1.0):
    m, k = a.shape
    bm = min(bm, m)
    bn = math.gcd(math.gcd(math.gcd(bn, n_size), n_start), scaled_cols)
    assert m % bm == 0 and bn % 128 == 0
    joff = n_start // bn
    return pl.pallas_call(
        functools.partial(_mm_kernel, scaled_blocks=scaled_cols // bn, scale=scale),
        out_shape=jax.ShapeDtypeStruct((m, n_size), out_dtype),
        grid=(m // bm, n_size // bn),
        in_specs=[pl.BlockSpec((bm, k), lambda i, j: (i, 0)),
                  pl.BlockSpec((k, bn), lambda i, j: (0, j + joff))],
        out_specs=pl.BlockSpec((bm, bn), lambda i, j: (i, j)),
        compiler_params=_cparams(2),
        name="matmul",
    )(a, w)


def _mm2_kernel(a1_ref, a2_ref, w_ref, o_ref):
    k1 = a1_ref.shape[1]
    acc = jnp.dot(a1_ref[...], w_ref[0:k1, :], preferred_element_type=jnp.float32)
    acc = acc + jnp.dot(a2_ref[...], w_ref[k1:, :], preferred_element_type=jnp.float32)
    o_ref[...] = acc.astype(o_ref.dtype)


def _matmul_cat(a1, a2, w, out_dtype, bm=1024, bn=1024):
    m, k1 = a1.shape
    k2 = a2.shape[1]
    n = w.shape[1]
    bm = min(bm, m)
    bn = min(bn, n)
    assert m % bm == 0 and n % bn == 0 and w.shape[0] == k1 + k2
    return pl.pallas_call(
        _mm2_kernel,
        out_shape=jax.ShapeDtypeStruct((m, n), out_dtype),
        grid=(m // bm, n // bn),
        in_specs=[pl.BlockSpec((bm, k1), lambda i, j: (i, 0)),
                  pl.BlockSpec((bm, k2), lambda i, j: (i, 0)),
                  pl.BlockSpec((k1 + k2, bn), lambda i, j: (0, j))],
        out_specs=pl.BlockSpec((bm, bn), lambda i, j: (i, j)),
        compiler_params=_cparams(2),
        name="matmul_cat",
    )(a1, a2, w)


def _layer_norm_rows(r, g, b):
    mu = jnp.mean(r, axis=-1, keepdims=True)
    d = r - mu
    var = jnp.mean(d * d, axis=-1, keepdims=True)
    return d * lax.rsqrt(var + LN_EPS) * g + b


def _mm_ln_pipelined_kernel(a_ref, w_ref, x_ref, g_ref, b_ref, of_ref, *rest):
    ob_ref = rest[0] if len(rest) == 3 else None
    y_refs = rest[-2:]
    i = pl.program_id(0)

    def multiply(y_ref):
        y_ref[...] = jnp.dot(a_ref[...], w_ref[...], preferred_element_type=jnp.float32)

    def normalise(y_ref):
        o = _layer_norm_rows(DEEPNORM_ALPHA * x_ref[...] + y_ref[...], g_ref[...], b_ref[...])
        of_ref[...] = o
        if ob_ref is not None:
            ob_ref[...] = o.astype(jnp.bfloat16)

    @pl.when(i == 0)
    def _():
        multiply(y_refs[0])

    for parity in (0, 1):
        @pl.when((i > 0) & (i % 2 == parity))
        def _():
            multiply(y_refs[parity])
            normalise(y_refs[1 - parity])


def _matmul_residual_ln(a, w, x, g, b, emit_bf16, bm=256):
    m, k = a.shape
    d = w.shape[1]
    bm = min(bm, m)
    n_tiles = m // bm
    lagged = pl.BlockSpec((bm, d), lambda i: (jnp.maximum(i - 1, 0), 0))
    vec = pl.BlockSpec((1, d), lambda i: (0, 0))
    out_shape = [jax.ShapeDtypeStruct((m, d), jnp.float32)]
    if emit_bf16:
        out_shape.append(jax.ShapeDtypeStruct((m, d), jnp.bfloat16))
    out = pl.pallas_call(
        _mm_ln_pipelined_kernel,
        out_shape=tuple(out_shape),
        grid=(n_tiles + 1,),
        in_specs=[pl.BlockSpec((bm, k), lambda i: (jnp.minimum(i, n_tiles - 1), 0)),
                  pl.BlockSpec((k, d), lambda i: (0, 0)), lagged, vec, vec],
        out_specs=tuple([lagged] * len(out_shape)),
        scratch_shapes=[pltpu.VMEM((bm, d), jnp.float32)] * 2,
        compiler_params=_cparams(1),
        name="matmul_residual_ln",
    )(a, w, x, g.reshape(1, d), b.reshape(1, d))
    return out if emit_bf16 else (out[0], None)


def _conv_proj_kernel(a_ref, wh_ref, wb_ref, wc_ref, wg_ref, cw_ref, *rest, tiles_per_seq, n_cast):
    cast_refs, o_ref = rest[:n_cast], rest[n_cast]
    cast_out_refs, tail_ref = rest[n_cast + 1:2 * n_cast + 1], rest[2 * n_cast + 1]
    i = pl.program_id(0)
    j = pl.program_id(1)
    for src_ref, dst_ref in zip(cast_refs, cast_out_refs):
        dst_ref[...] = src_ref[...].astype(dst_ref.dtype)
    a = a_ref[...]
    u = (jnp.dot(a, wc_ref[...], preferred_element_type=jnp.float32)
         * jnp.dot(a, wh_ref[...], preferred_element_type=jnp.float32))
    bm = u.shape[0]

    @pl.when(i % tiles_per_seq == 0)
    def _():
        tail_ref[j] = jnp.zeros(tail_ref.shape[1:], jnp.float32)

    prev = tail_ref[j]
    tail_ref[j] = u[bm - 8:bm, :]
    row = lax.broadcasted_iota(jnp.int32, u.shape, 0)
    u1 = jnp.where(row == 0, prev[7:8, :], pltpu.roll(u, 1, 0))
    u2 = jnp.where(row == 0, prev[6:7, :],
                   jnp.where(row == 1, prev[7:8, :], pltpu.roll(u, 2, 0)))
    w = cw_ref[...]
    conv = w[0:1, :] * u2 + w[1:2, :] * u1 + w[2:3, :] * u
    b = jnp.dot(a, wb_ref[...], preferred_element_type=jnp.float32)
    g = jnp.dot(a, wg_ref[...], preferred_element_type=jnp.float32)
    o_ref[...] = (b * conv * _silu(g)).astype(o_ref.dtype)


def _conv_proj(a, w_in, conv_w, width, seq_len, cast_srcs, bm=1024, bc=256):
    m, k = a.shape
    bm = min(bm, seq_len)
    bc = min(bc, width)
    nc = width // bc
    steps = (m // bm) * nc
    assert seq_len % bm == 0 and m % seq_len == 0 and width % bc == 0 and bm % 8 == 0

    def wcol(group):
        return pl.BlockSpec((k, bc), lambda i, j: (0, j + group * nc))

    bf16_rows = 16
    flat = []
    for src in cast_srcs:
        rows = src.shape[0] if src.shape[0] % (steps * bf16_rows) == 0 else steps * bf16_rows
        assert src.size % (rows * 128) == 0
        flat.append(src.reshape(rows, src.size // rows))
    cast_specs = [pl.BlockSpec((f.shape[0] // steps, f.shape[1]), lambda i, j: (i * nc + j, 0))
                  for f in flat]
    out = pl.pallas_call(
        functools.partial(_conv_proj_kernel, tiles_per_seq=seq_len // bm, n_cast=len(flat)),
        out_shape=(jax.ShapeDtypeStruct((m, width), jnp.bfloat16),
                   *[jax.ShapeDtypeStruct(f.shape, jnp.bfloat16) for f in flat]),
        grid=(m // bm, nc),
        in_specs=[pl.BlockSpec((bm, k), lambda i, j: (i, 0)),
                  wcol(0), wcol(1), wcol(2), wcol(3),
                  pl.BlockSpec((CONV_K, bc), lambda i, j: (0, j)), *cast_specs],
        out_specs=(pl.BlockSpec((bm, bc), lambda i, j: (i, j)), *cast_specs),
        scratch_shapes=[pltpu.VMEM((nc, 8, bc), jnp.float32)],
        compiler_params=_cparams(2),
        name="conv_proj",
    )(a, w_in, w_in, w_in, w_in, conv_w, *flat)
    return out[0], [o.reshape(src.shape) for o, src in zip(out[1:], cast_srcs)]


def _t5_bucket_thresholds():
    d = np.arange(0, 4 * REL_MAX_DIST)
    max_exact = REL_BUCKETS // 2
    nf = np.maximum(d, 1).astype(np.float32)
    large = max_exact + (np.log(nf / np.float32(max_exact)) / np.float32(math.log(REL_MAX_DIST / max_exact))
                         * np.float32(REL_BUCKETS - max_exact)).astype(np.int32)
    large = np.minimum(large, REL_BUCKETS - 1)
    bucket = np.where(d < max_exact, d, large)
    assert np.all(np.diff(bucket) >= 0) and bucket[-1] == REL_BUCKETS - 1
    return [int(np.argmax(bucket >= b)) for b in range(REL_BUCKETS)]


_BUCKET_THR = _t5_bucket_thresholds()
_MOBA_BLOCK_SHIFT = MOBA_BLOCK.bit_length() - 1
assert 1 << _MOBA_BLOCK_SHIFT == MOBA_BLOCK


def _moba_kernel(rb_ref, q_ref, k_ref, v_ref, g_ref, o_ref, t0_ref, t1_ref, kaug_ref, vaug_ref, *,
                 nb):
    blk = MOBA_BLOCK
    s_len = nb * blk
    heads = range(MOBA_HEADS_PER_STEP)
    head_ids = [pl.program_id(0) * MOBA_HEADS_PER_STEP + hh for hh in heads]
    cols = [slice(hh * HEAD_DIM, (hh + 1) * HEAD_DIM) for hh in heads]
    qo = lax.broadcasted_iota(jnp.int32, (blk, blk), 0)
    ko = lax.broadcasted_iota(jnp.int32, (blk, blk), 1)

    @pl.when(pl.program_id(1) == 0)
    def _():
        def table(h, dist):
            t = jnp.full(dist.shape, rb_ref[h, REL_BUCKETS - 1], jnp.float32)
            for bkt in range(REL_BUCKETS - 2, -1, -1):
                t = jnp.where(dist < _BUCKET_THR[bkt + 1], rb_ref[h, bkt], t)
            return t * LOG2E
        for hh in heads:
            t0_ref[hh] = table(head_ids[hh], qo - ko)
            t1_ref[hh] = table(head_ids[hh], qo - ko + blk)

    rb_far = [rb_ref[h, REL_BUCKETS - 1] * LOG2E for h in head_ids]
    assert _BUCKET_THR[REL_BUCKETS - 1] <= blk + 1

    block_onehot = jnp.where(
        jnp.right_shift(lax.broadcasted_iota(jnp.int32, (s_len, HEAD_DIM), 0), _MOBA_BLOCK_SHIFT)
        == lax.broadcasted_iota(jnp.int32, (s_len, HEAD_DIM), 1), 1.0, 0.0).astype(jnp.bfloat16)
    kbars = []
    for hh in heads:
        kaug_ref[hh, :, 0:HEAD_DIM] = k_ref[0, :, cols[hh]]
        kaug_ref[hh, :, HEAD_DIM:2 * HEAD_DIM] = block_onehot
        kbar = jnp.concatenate(
            [jnp.mean(k_ref[0, j * blk:(j + 1) * blk, cols[hh]].astype(jnp.float32),
                      axis=0, keepdims=True) for j in range(nb)]
            + [jnp.zeros((16 - nb, HEAD_DIM), jnp.float32)], axis=0)
        kbars.append(_split_bf16(kbar))
        vaug_ref[hh, :, 0:HEAD_DIM] = v_ref[0, :, cols[hh]]
        vaug_ref[hh, :, HEAD_DIM:2 * HEAD_DIM] = jnp.ones((s_len, HEAD_DIM), jnp.bfloat16)

    eye = jnp.where(qo == ko, 1.0, 0.0).astype(jnp.bfloat16)
    jidx = lax.broadcasted_iota(jnp.int32, (16, blk), 0)

    def logits(hh, i):
        q_i = q_ref[0, i * blk:(i + 1) * blk, cols[hh]]
        kbar_hi, kbar_lo = kbars[hh]
        nk = (i + 1) * blk
        if i > MOBA_TOPK:
            gate = (lax.dot_general(kbar_hi, q_i, _NT, preferred_element_type=jnp.float32)
                    + lax.dot_general(kbar_lo, q_i, _NT, preferred_element_type=jnp.float32))
            cnt = jnp.zeros((16, blk), jnp.float32)
            for jp in range(i):
                row = gate[jp:jp + 1, :]
                beats = jnp.where(row > gate, 1.0,
                                  jnp.where(row == gate, jnp.where(jidx > jp, 1.0, 0.0), 0.0))
                cnt = cnt + beats
            drop_t = jnp.where(cnt < MOBA_TOPK, 0.0, jnp.where(jidx < i, NEG_INF, 0.0))
            drop_t = jnp.concatenate([drop_t, jnp.zeros((128 - 16, blk), jnp.float32)], axis=0)
            drop = lax.dot_general(eye, drop_t.astype(jnp.bfloat16), _NT,
                                   preferred_element_type=jnp.float32)
            q_aug = jnp.concatenate([q_i, drop.astype(jnp.bfloat16)], axis=-1)
            return lax.dot_general(q_aug, kaug_ref[hh, 0:nk, :], _NT,
                                   preferred_element_type=jnp.float32)
        return lax.dot_general(q_i, k_ref[0, 0:nk, cols[hh]], _NT,
                               preferred_element_type=jnp.float32)

    def attend(hh, i, s):
        nk = (i + 1) * blk
        n_far = max(i - 1, 0) * blk
        lg_own = jnp.where(ko <= qo, s[:, i * blk:nk] + t0_ref[hh], NEG_INF)
        m = jnp.max(lg_own, axis=-1, keepdims=True)
        if i > 0:
            lg_prev = s[:, (i - 1) * blk:i * blk] + t1_ref[hh]
            m = jnp.maximum(m, jnp.max(lg_prev, axis=-1, keepdims=True))
        if n_far:
            m = jnp.maximum(m, jnp.max(s[:, 0:n_far], axis=-1, keepdims=True) + rb_far[hh])
        pieces = [jnp.exp2(lg_own - m)]
        if i > 0:
            pieces.insert(0, jnp.exp2(lg_prev - m))
        if n_far:
            pieces.insert(0, jnp.exp2(s[:, 0:n_far] - (m - rb_far[hh])))
        p = pieces[0] if len(pieces) == 1 else jnp.concatenate(pieces, axis=-1)
        pv = jnp.dot(p.astype(jnp.bfloat16), vaug_ref[hh, 0:nk, :],
                     preferred_element_type=jnp.float32)
        o = pv[:, 0:HEAD_DIM] / pv[:, HEAD_DIM:HEAD_DIM + 1]
        rows = slice(i * blk, (i + 1) * blk)
        o_ref[0, rows, cols[hh]] = (o * _silu(g_ref[0, rows, cols[hh]])).astype(o_ref.dtype)

    items = [(hh, i) for hh in heads for i in range(nb)]
    s_next = logits(*items[0])
    for n, item in enumerate(items):
        s_cur, s_next = s_next, (logits(*items[n + 1]) if n + 1 < len(items) else None)
        attend(*item, s_cur)


def _moba(qkv, g, rel_bias_t, heads):
    bsz, s, _ = qkv.shape
    assert s % MOBA_BLOCK == 0
    nb = s // MOBA_BLOCK
    assert MOBA_TOPK < nb <= 16 and heads % MOBA_HEADS_PER_STEP == 0
    steps = heads // MOBA_HEADS_PER_STEP
    width = MOBA_HEADS_PER_STEP * HEAD_DIM

    def col(group):
        return pl.BlockSpec((1, s, width), lambda h, b: (b, 0, h + group * steps))

    return pl.pallas_call(
        functools.partial(_moba_kernel, nb=nb),
        out_shape=jax.ShapeDtypeStruct((bsz, s, heads * HEAD_DIM), jnp.bfloat16),
        grid=(steps, bsz),
        in_specs=[pl.BlockSpec(memory_space=pltpu.SMEM), col(0), col(1), col(2), col(0)],
        out_specs=col(0),
        scratch_shapes=[pltpu.VMEM((MOBA_HEADS_PER_STEP, MOBA_BLOCK, MOBA_BLOCK), jnp.float32),
                        pltpu.VMEM((MOBA_HEADS_PER_STEP, MOBA_BLOCK, MOBA_BLOCK), jnp.float32),
                        pltpu.VMEM((MOBA_HEADS_PER_STEP, s, 2 * HEAD_DIM), jnp.bfloat16),
                        pltpu.VMEM((MOBA_HEADS_PER_STEP, s, 2 * HEAD_DIM), jnp.bfloat16)],
        compiler_params=_cparams(2),
        name="moba",
    )(rel_bias_t, qkv, qkv, qkv, g)


def _sb_kernel(q_ref, k_ref, v_ref, g_ref, o_ref, za_ref, zb_ref, acc_ref, car_ref, *, nq):
    t = SB_BLOCK
    tq = 2 * t
    tri = jnp.where(lax.broadcasted_iota(jnp.int32, (t, t), 0)
                    > lax.broadcasted_iota(jnp.int32, (t, t), 1), 1.0, 0.0).astype(jnp.bfloat16)

    def logits(qs, j0):
        return tuple(lax.dot_general(q, k_ref[0, pl.ds(j0, t), hh * HEAD_DIM:(hh + 1) * HEAD_DIM],
                                     _NT, preferred_element_type=jnp.float32)
                     for hh, q in enumerate(qs))

    def tiles(zs, j0, rows, diag):
        heads = range(len(zs))
        cols = [slice(hh * HEAD_DIM, (hh + 1) * HEAD_DIM) for hh in heads]
        if diag:
            strict = (lax.broadcasted_iota(jnp.int32, zs[0].shape, 1)
                      < lax.broadcasted_iota(jnp.int32, zs[0].shape, 0))
            zs = [jnp.where(strict, z, NEG_INF) for z in zs]
        zs = [jnp.minimum(z, SB_MAX_LOG2_ODDS) for z in zs]
        sps = [jnp.log(1.0 + jnp.exp2(z)) * LOG2E for z in zs]
        rems = [jnp.dot(sps[hh].astype(jnp.bfloat16), tri, preferred_element_type=jnp.float32)
                + car_ref[hh, rows, :] for hh in heads]
        ws = [jnp.exp2((zs[hh] - sps[hh]) - rems[hh]).astype(jnp.bfloat16) for hh in heads]
        for hh in heads:
            acc_ref[hh, rows, :] += jnp.dot(ws[hh], v_ref[0, pl.ds(j0, t), cols[hh]],
                                            preferred_element_type=jnp.float32)
            car_ref[hh, rows, :] = rems[hh][:, 0:1] + sps[hh][:, 0:1]

    def q_block(i, _):
        q0 = pl.multiple_of(i * tq, tq)
        q1 = pl.multiple_of(i * tq + t, t)
        heads = range(SB_HEADS_PER_STEP)
        cols = [slice(hh * HEAD_DIM, (hh + 1) * HEAD_DIM) for hh in heads]
        q_i = [q_ref[0, pl.ds(q0, tq), c] for c in cols]
        acc_ref[...] = jnp.zeros(acc_ref.shape, jnp.float32)
        car_ref[...] = jnp.zeros(car_ref.shape, jnp.float32)

        def past_j0(n):
            return pl.multiple_of(jnp.maximum(2 * i - 1 - n, 0) * t, t)

        def stage_logits(z_ref, n, rows):
            for hh, z in enumerate(logits([q[rows, :] for q in q_i], past_j0(n))):
                z_ref[hh, rows, :] = z

        zs_low = logits([q_ref[0, pl.ds(q1, t), c] for c in cols], q1)
        zs_diag = logits(q_i, q0)
        stage_logits(za_ref, 0, slice(0, tq))
        tiles(zs_low, q1, slice(t, tq), True)
        tiles(zs_diag, q0, slice(0, tq), True)

        def all_dead(rows):
            return (jnp.min(jnp.min(car_ref[:, rows, :], axis=0)) >= SB_DEAD_BITS).astype(jnp.int32)

        def walk(rows):
            def past_pair(carry):
                m = carry[0]
                stage_logits(zb_ref, 2 * m + 1, rows)
                tiles([za_ref[hh, rows, :] for hh in heads], past_j0(2 * m), rows, False)

                @pl.when(all_dead(rows) == 0)
                def _():
                    stage_logits(za_ref, 2 * m + 2, rows)
                    tiles([zb_ref[hh, rows, :] for hh in heads], past_j0(2 * m + 1), rows, False)

                return m + 1, all_dead(rows)

            lax.while_loop(lambda carry: (carry[0] < i) & (carry[1] == 0), past_pair,
                           (jnp.int32(0), jnp.int32(0)))

        lower_dead = all_dead(slice(t, tq))

        @pl.when(lower_dead == 1)
        def _():
            walk(slice(0, t))

        @pl.when(lower_dead == 0)
        def _():
            walk(slice(0, tq))
        acc = jnp.concatenate([acc_ref[hh] for hh in heads], axis=-1)
        o_ref[0, pl.ds(q0, tq), :] = (acc * _silu(g_ref[0, pl.ds(q0, tq), :])).astype(o_ref.dtype)
        return 0

    lax.fori_loop(0, nq, q_block, 0)


def _stick_breaking(qkv, g, heads):
    bsz, s, _ = qkv.shape
    assert s % (2 * SB_BLOCK) == 0 and heads % SB_HEADS_PER_STEP == 0
    steps = heads // SB_HEADS_PER_STEP
    width = SB_HEADS_PER_STEP * HEAD_DIM

    def col(group):
        return pl.BlockSpec((1, s, width), lambda h, b: (b, 0, h + group * steps))

    return pl.pallas_call(
        functools.partial(_sb_kernel, nq=s // (2 * SB_BLOCK)),
        out_shape=jax.ShapeDtypeStruct((bsz, s, heads * HEAD_DIM), jnp.bfloat16),
        grid=(steps, bsz),
        in_specs=[col(0), col(1), col(2), col(0)],
        out_specs=col(0),
        scratch_shapes=[pltpu.VMEM((SB_HEADS_PER_STEP, 2 * SB_BLOCK, SB_BLOCK), jnp.float32)] * 2
        + [pltpu.VMEM((SB_HEADS_PER_STEP, 2 * SB_BLOCK, HEAD_DIM), jnp.float32),
           pltpu.VMEM((SB_HEADS_PER_STEP, 2 * SB_BLOCK, 1), jnp.float32)],
        compiler_params=_cparams(2),
        name="stick_breaking",
    )(qkv, qkv, qkv, g)


def _xattn_kernel(q_ref, g_ref, k_ref, v_ref, o_ref, *, dh):
    scale = dh ** -0.5
    for h in range(XATTN_HEADS):
        cols = slice(h * dh, (h + 1) * dh)
        s = lax.dot_general(q_ref[0, :, cols], k_ref[0, :, cols], _NT,
                            preferred_element_type=jnp.float32) * scale
        m = jnp.max(s, axis=-1, keepdims=True)
        p = jnp.exp(s - m)
        l = jnp.sum(p, axis=-1, keepdims=True)
        o = jnp.dot(p.astype(jnp.bfloat16), v_ref[0, :, cols],
                    preferred_element_type=jnp.float32) / l
        o_ref[0, :, cols] = (o * _silu(g_ref[0, :, cols])).astype(o_ref.dtype)


def _xattn(q, g, kv, bm=512):
    bsz, s, w = q.shape
    n_mem = kv.shape[1]
    bm = min(bm, s)
    row = pl.BlockSpec((1, bm, w), lambda b, i: (b, i, 0))
    return pl.pallas_call(
        functools.partial(_xattn_kernel, dh=w // XATTN_HEADS),
        out_shape=jax.ShapeDtypeStruct((bsz, s, w), jnp.bfloat16),
        grid=(bsz, s // bm),
        in_specs=[row, row,
                  pl.BlockSpec((1, n_mem, w), lambda b, i: (b, 0, 0)),
                  pl.BlockSpec((1, n_mem, w), lambda b, i: (b, 0, 1))],
        out_specs=row,
        compiler_params=_cparams(2),
        name="xattn",
    )(q, g, kv, kv)


def _ln_proj_kernel(x_ref, y_ref, g_ref, b_ref, w_ref, xf_ref, q_ref, gate_ref, a0_ref, a1_ref, *,
                    q_blocks):
    i = pl.program_id(0)
    j = pl.program_id(1)
    chunk = x_ref.shape[0]
    a_refs = (a0_ref, a1_ref)

    def normalise(a_ref):
        o = _layer_norm_rows(DEEPNORM_ALPHA * x_ref[...] + y_ref[...], g_ref[...], b_ref[...])
        xf_ref[...] = o
        a_ref[pl.ds(pl.multiple_of(j * chunk, chunk), chunk), :] = o.astype(jnp.bfloat16)

    def emit(acc):
        @pl.when(j < q_blocks)
        def _():
            q_ref[...] = acc.astype(q_ref.dtype)

        @pl.when(j >= q_blocks)
        def _():
            gate_ref[...] = acc

    @pl.when(i == 0)
    def _():
        normalise(a_refs[0])

    for parity in (0, 1):
        @pl.when((i > 0) & (i % 2 == parity))
        def _():
            acc = jnp.dot(a_refs[1 - parity][...], w_ref[...], preferred_element_type=jnp.float32)
            normalise(a_refs[parity])
            emit(acc)


def _residual_ln_proj(x, y, g, b, w, q_cols, bm=1024, bn=256):
    m, d = x.shape
    n = w.shape[1]
    bm = min(bm, m)
    bn = math.gcd(bn, q_cols, n - q_cols)
    nj = n // bn
    chunk = bm // nj
    n_tiles = m // bm
    q_blocks = q_cols // bn
    assert m % bm == 0 and bm % nj == 0 and chunk % 16 == 0 and 0 < q_blocks < nj

    def chunk_idx(i, j):
        return (jnp.where(i < n_tiles, i * nj + j, n_tiles * nj - 1), 0)

    def out_idx(col):
        return lambda i, j: (jnp.maximum(i - 1, 0), jnp.where(i == 0, 0, col(j)))

    rows = pl.BlockSpec((chunk, d), chunk_idx)
    vec = pl.BlockSpec((1, d), lambda i, j: (0, 0))
    return pl.pallas_call(
        functools.partial(_ln_proj_kernel, q_blocks=q_blocks),
        out_shape=(jax.ShapeDtypeStruct((m, d), jnp.float32),
                   jax.ShapeDtypeStruct((m, q_cols), jnp.bfloat16),
                   jax.ShapeDtypeStruct((m, n - q_cols), jnp.float32)),
        grid=(n_tiles + 1, nj),
        in_specs=[rows, rows, vec, vec, pl.BlockSpec((d, bn), lambda i, j: (0, j))],
        out_specs=(rows,
                   pl.BlockSpec((bm, bn), out_idx(lambda j: jnp.minimum(j, q_blocks - 1))),
                   pl.BlockSpec((bm, bn), out_idx(lambda j: jnp.maximum(j - q_blocks, 0)))),
        scratch_shapes=[pltpu.VMEM((bm, d), jnp.bfloat16)] * 2,
        compiler_params=_cparams(2),
        name="residual_ln_proj",
    )(x, y, g.reshape(1, d), b.reshape(1, d), w)


def _memory_block(x_f32, y, g1, b1, mem_bf16, w_xq, w_xkv, w_xo, g2, b2, bsz, s, emit_bf16):
    xw = w_xo.shape[0]
    x_f32, q, g = _residual_ln_proj(x_f32, y, g1, b1, w_xq, xw)
    kv = _matmul(mem_bf16, w_xkv, 0, 2 * xw, jnp.bfloat16)
    n_mem = mem_bf16.shape[0] // bsz
    o = _xattn(q.reshape(bsz, s, xw), g.reshape(bsz, s, xw), kv.reshape(bsz, n_mem, 2 * xw))
    return _matmul_residual_ln(o.reshape(bsz * s, xw), w_xo, x_f32, g2, b2, emit_bf16)


def kernel(x, mem, rel_bias, w_in_0, conv_w_0, w_out_0, ln1_g_0, ln1_b_0, w_xq_0, w_xkv_0, w_xo_0, ln2_g_0, ln2_b_0, w_in_1, w_out_1, ln1_g_1, ln1_b_1, w_xq_1, w_xkv_1, w_xo_1, ln2_g_1, ln2_b_1):
    bsz, s, d = x.shape
    bf = jnp.bfloat16
    xf = x.reshape(bsz * s, d)
    xb = xf.astype(bf)
    memb = mem.reshape(-1, d).astype(bf)

    cw = conv_w_0.shape[1]
    mw = w_in_0.shape[1] // 4 - cw
    heads = mw // HEAD_DIM
    w_in = w_in_0.astype(bf)
    later = [w_out_0, w_xq_0, w_xkv_0, w_xo_0, w_in_1, w_out_1, w_xq_1, w_xkv_1, w_xo_1]
    y_a, later = _conv_proj(xb, w_in, conv_w_0, cw, s, later)
    w_out_0, w_xq_0, w_xkv_0, w_xo_0, w_in_1, w_out_1, w_xq_1, w_xkv_1, w_xo_1 = later
    qkv = _matmul(xb, w_in, 4 * cw, 3 * mw, bf, scaled_cols=mw, scale=HEAD_DIM ** -0.5 * LOG2E)
    g_b = _matmul(xb, w_in, 4 * cw + 3 * mw, mw, jnp.float32)
    y_b = _moba(qkv.reshape(bsz, s, 3 * mw), g_b.reshape(bsz, s, mw), rel_bias.T, heads)
    y = _matmul_cat(y_a, y_b.reshape(bsz * s, mw), w_out_0, jnp.float32)
    xf, xb = _memory_block(xf, y, ln1_g_0, ln1_b_0, memb, w_xq_0, w_xkv_0, w_xo_0,
                           ln2_g_0, ln2_b_0, bsz, s, True)

    sw = w_out_1.shape[0]
    qkv = _matmul(xb, w_in_1, 0, 3 * sw, bf, scaled_cols=sw, scale=HEAD_DIM ** -0.5 * LOG2E)
    g_c = _matmul(xb, w_in_1, 3 * sw, sw, jnp.float32)
    y_c = _stick_breaking(qkv.reshape(bsz, s, 3 * sw), g_c.reshape(bsz, s, sw), sw // HEAD_DIM)
    y = _matmul(y_c.reshape(bsz * s, sw), w_out_1, 0, d, jnp.float32)
    xf, _ = _memory_block(xf, y, ln1_g_1, ln1_b_1, memb, w_xq_1, w_xkv_1, w_xo_1,
                          ln2_g_1, ln2_b_1, bsz, s, False)
    return xf.reshape(bsz, s, d)
```
